```python
import math
import jax, jax.numpy as jnp
from jax import lax
import numpy as np


D_MODEL = 2048
BATCH = 4
SEQ = 2048
DEPTH = 4
DEC_BATCH = 8
DEC_SEQ = 8
PAST_LEN = 16384
PAGE_SIZE = 128

D_MIX = D_MODEL
HG_WIDTH = D_MIX // 4
HG_DK = 128
HG_H = HG_WIDTH // HG_DK
HG_DV = HG_WIDTH // HG_H
HGRN_CHUNK = 64
RW_WIDTH = D_MIX // 4
RW_N = 64
RW_H = RW_WIDTH // RW_N
LORA_W = D_MODEL // 32
LORA_A = D_MODEL // 32
LORA_G = D_MODEL // 16
RW_GN_EPS = 64e-5
FX_WIDTH = D_MIX - HG_WIDTH - RW_WIDTH
FX_DH = 128
FX_H = FX_WIDTH // FX_DH
Q_BLOCK = 128
D_FF = ((8 * D_MODEL // 3 + 127) // 128) * 128
N_EXPERTS = 8
TOP_K = 2
D_FF_EXPERT = 7 * D_MODEL // 2
N_DENSE = (DEPTH + 1) // 2
N_MOE = DEPTH // 2
EPS = 1e-6

HG_SPLITS = (HG_WIDTH, HG_WIDTH, HG_WIDTH, HG_WIDTH)
RW_SPLITS = (RW_WIDTH, LORA_W, RW_WIDTH, RW_WIDTH, LORA_A, LORA_G)
C_RWKV = sum(RW_SPLITS)
FX_SPLITS = (FX_WIDTH, FX_WIDTH, FX_WIDTH, FX_H)
IN_SPLITS = HG_SPLITS + (C_RWKV,) + FX_SPLITS
C_IN = sum(IN_SPLITS)

kernel_name = "hymba_hgrn2_rwkv7_fox_moe_step"


def split_cols(x, sizes):
    bounds = [int(b) for b in np.cumsum(sizes)[:-1]]
    return jnp.split(x, bounds, axis=-1)


def rms_norm(x, g):
    xf = x.astype(jnp.float32)
    y = xf * lax.rsqrt(jnp.mean(xf * xf, axis=-1, keepdims=True) + EPS)
    return (y * g.astype(jnp.float32)).astype(x.dtype)


def swiglu(u, w1, w3, w2):
    return (jax.nn.silu(u @ w1) * (u @ w3)) @ w2


def hgrn2_chunked(q, k, v, lf, s0):
    B, T, H, DK = q.shape
    DV = v.shape[-1]
    C = math.gcd(T, HGRN_CHUNK)
    n = T // C
    to_chunks = lambda t: t.reshape(B, n, C, H, t.shape[-1]).transpose(1, 0, 3, 2, 4)
    tri = jnp.tril(jnp.ones((C, C), dtype=bool))[:, :, None]

    def step(S, inp):
        qc, kc, vc, lfc = inp
        G = jnp.cumsum(lfc, axis=2)
        diff = G[:, :, :, None, :] - G[:, :, None, :, :]
        decay = jnp.where(tri, jnp.exp(jnp.where(tri, diff, 0.0)), 0.0)
        A = jnp.einsum('bhtd,bhsd,bhtsd->bhts', qc, kc, decay)
        o = jnp.einsum('bhts,bhsv->bhtv', A, vc) + jnp.einsum('bhtd,bhdv->bhtv', qc * jnp.exp(G), S)
        g_last = G[:, :, -1:, :]
        S = jnp.exp(g_last[:, :, 0, :])[..., None] * S + jnp.einsum(
            'bhsd,bhsv->bhdv', kc * jnp.exp(g_last - G), vc)
        return S, o

    S, o = lax.scan(step, s0.astype(jnp.float32), (to_chunks(q), to_chunks(k), to_chunks(v), to_chunks(lf)))
    o = o.transpose(1, 0, 3, 2, 4).reshape(B, T, H, DV)
    return o, S


def hgrn2_mix(q, f_logit, i, g, lb, s0, norm_w):
    B, T, _ = q.shape
    heads_k = lambda t: t.astype(jnp.float32).reshape(B, T, HG_H, HG_DK)
    qh = jax.nn.silu(heads_k(q))
    fl = heads_k(f_logit)
    lbh = lb.astype(jnp.float32).reshape(HG_H, HG_DK)
    lf = jnp.logaddexp(jnp.log(lbh), jnp.log1p(-lbh) + jax.nn.log_sigmoid(fl))
    kh = (1.0 - lbh) * jax.nn.sigmoid(-fl)
    vh = i.astype(jnp.float32).reshape(B, T, HG_H, HG_DV)
    o, S = hgrn2_chunked(qh, kh, vh, lf, s0)
    gate = jax.nn.silu(g.astype(jnp.float32).reshape(B, T, HG_H, HG_DV))
    o = rms_norm(o, norm_w) * gate
    return o.reshape(B, T, HG_WIDTH).astype(q.dtype), S.astype(s0.dtype)


def rwkv7_mix(p, shift0, s0, mu, w0, w2, a0, a2, g2, k_k, k_a, r_k, lnx_w, lnx_b):
    B, T, _ = p.shape
    prev = jnp.concatenate([shift0[:, None, :].astype(p.dtype), p[:, :-1]], axis=1)
    xs = p + (prev - p) * mu
    r, wd, k, v, ad, gd = split_cols(xs, RW_SPLITS)
    w = (w0 + jnp.tanh(wd) @ w2).astype(jnp.float32)
    logw = -jax.nn.softplus(-w) - 0.5
    decay = jnp.exp(-jnp.exp(logw))
    a = jax.nn.sigmoid((a0 + ad @ a2).astype(jnp.float32))
    gate = (jax.nn.sigmoid(gd) @ g2).astype(jnp.float32)
    heads = lambda t: t.astype(jnp.float32).reshape(B, T, RW_H, RW_N)
    r, k, v, a, decay = heads(r), heads(k), heads(v), heads(a), heads(decay)
    kk = k * k_k.astype(jnp.float32).reshape(RW_H, RW_N)
    kk = kk / jnp.maximum(jnp.sqrt(jnp.sum(kk * kk, axis=-1, keepdims=True)), 1e-12)
    k = k * (1.0 + (a - 1.0) * k_a.astype(jnp.float32).reshape(RW_H, RW_N))

    def step(S, inp):
        r_t, w_t, k_t, v_t, kk_t, a_t = inp
        sa = jnp.einsum('bhvk,bhk->bhv', S, -kk_t)
        S = S * w_t[:, :, None, :] + sa[..., None] * (kk_t * a_t)[:, :, None, :] \
            + v_t[..., None] * k_t[:, :, None, :]
        return S, jnp.einsum('bhvk,bhk->bhv', S, r_t)

    tm = lambda t: t.transpose(1, 0, 2, 3)
    S, y = lax.scan(step, s0.astype(jnp.float32), (tm(r), tm(decay), tm(k), tm(v), tm(kk), tm(a)))
    y = y.transpose(1, 0, 2, 3)
    mean = jnp.mean(y, axis=-1, keepdims=True)
    var = jnp.mean(jnp.square(y - mean), axis=-1, keepdims=True)
    yn = ((y - mean) * lax.rsqrt(var + RW_GN_EPS)).reshape(B, T, RW_WIDTH)
    yn = yn * lnx_w.astype(jnp.float32) + lnx_b.astype(jnp.float32)
    bonus = (jnp.sum(r * k * r_k.astype(jnp.float32), axis=-1, keepdims=True) * v).reshape(B, T, RW_WIDTH)
    out = (yn + bonus) * gate
    return out.astype(p.dtype), S.astype(s0.dtype), p[:, -1].astype(shift0.dtype)


def fox_attend(q, K, V, cum):
    B, Tq, H, D = q.shape
    Tk = K.shape[1]
    off = Tk - Tq
    blk = Q_BLOCK if Tq % Q_BLOCK == 0 else Tq
    nb = Tq // blk
    scale = FX_DH ** -0.5
    ck = cum.transpose(0, 2, 1)[:, :, None, :]
    k_pos = jnp.arange(Tk)

    def block(i):
        start = i * blk
        qb = lax.dynamic_slice_in_dim(q, start, blk, axis=1)
        cq = lax.dynamic_slice_in_dim(cum, off + start, blk, axis=1).transpose(0, 2, 1)[..., None]
        s = jnp.einsum('bqhd,bkhd->bhqk', qb, K).astype(jnp.float32) * scale + (cq - ck)
        q_pos = off + start + jnp.arange(blk)
        s = jnp.where(k_pos[None, :] <= q_pos[:, None], s, -jnp.inf)
        pr = jax.nn.softmax(s, axis=-1)
        return jnp.einsum('bhqk,bkhd->bqhd', pr.astype(V.dtype), V)

    o = lax.map(block, jnp.arange(nb))
    return o.transpose(1, 0, 2, 3, 4).reshape(B, Tq, H, D)


def fox_mix(q, k, v, f_logit, past_k, past_v, past_lf, q_gain, k_gain, f_bias):
    B, T, _ = q.shape
    q = rms_norm(q.reshape(B, T, FX_H, FX_DH), q_gain)
    k = rms_norm(k.reshape(B, T, FX_H, FX_DH), k_gain)
    v = v.reshape(B, T, FX_H, FX_DH)
    lf = jax.nn.log_sigmoid((f_logit + f_bias).astype(jnp.float32))
    K = jnp.concatenate([past_k.astype(k.dtype), k], axis=1)
    V = jnp.concatenate([past_v.astype(v.dtype), v], axis=1)
    LF = jnp.concatenate([past_lf.astype(jnp.float32), lf], axis=1)
    cum = jnp.cumsum(LF, axis=1)
    o = fox_attend(q, K, V, cum)
    return (o.reshape(B, T, FX_WIDTH).astype(q.dtype), k.astype(past_k.dtype),
            v.astype(past_v.dtype), lf.astype(past_lf.dtype))


def moe_swiglu(u, router, w1, w3, w2):
    logits = (u @ router).astype(jnp.float32)
    top_v, top_i = lax.top_k(logits, TOP_K)
    gates = jax.nn.softmax(top_v, axis=-1)
    dense_gate = jnp.sum(gates[..., None] * jax.nn.one_hot(top_i, N_EXPERTS, dtype=jnp.float32), axis=-2)
    out = jnp.zeros(u.shape, jnp.float32)
    for e in range(N_EXPERTS):
        out = out + dense_gate[..., e:e + 1] * swiglu(u, w1[e], w3[e], w2[e]).astype(jnp.float32)
    return out.astype(u.dtype)


def layer(h, lw, l, hg0, rw0, sh0, past_k, past_v, past_lf):
    u = rms_norm(h, lw['norm_mix'])
    proj = u @ lw['w_in']
    hq, hf, hi, hg, p_rw, fq, fk, fv, ff = split_cols(proj, IN_SPLITS)
    o_hg, s_hg = hgrn2_mix(hq, hf, hi, hg, lw['lb'], hg0, lw['hgrn_norm'])
    o_rw, s_rw, sh = rwkv7_mix(p_rw, sh0, rw0, lw['rwkv_mu'], lw['rwkv_w0'], lw['rwkv_w2'], lw['rwkv_a0'],
                               lw['rwkv_a2'], lw['rwkv_g2'], lw['rwkv_kk'], lw['rwkv_ka'], lw['rwkv_rk'],
                               lw['rwkv_lnx_w'], lw['rwkv_lnx_b'])
    o_fx, k_new, v_new, lf_new = fox_mix(fq, fk, fv, ff, past_k, past_v, past_lf,
                                         lw['fox_qnorm'], lw['fox_knorm'], lw['fox_fbias'])
    h = h + jnp.concatenate([o_hg, o_rw, o_fx], axis=-1) @ lw['w_out']
    u2 = rms_norm(h, lw['norm_ffn'])
    if l % 2 == 0:
        h = h + swiglu(u2, lw['ffn_w1'], lw['ffn_w3'], lw['ffn_w2'])
    else:
        h = h + moe_swiglu(u2, lw['moe_router'], lw['moe_w1'], lw['moe_w3'], lw['moe_w2'])
    return h, k_new, v_new, lf_new, s_hg, s_rw, sh


def setup_inputs(seed: int = 0) -> dict:
    key = jax.random.key(seed)
    ks = jax.random.split(key, 40)
    nrm = lambda k, shape, scale: jax.random.normal(k, shape, jnp.float32) * scale
    n_pages = PAST_LEN // PAGE_SIZE
    n_used = DEC_BATCH * n_pages
    n_pool = n_used + max(1, n_used // 4)
    page_table = jax.random.permutation(ks[0], n_pool)[:n_used].reshape(DEC_BATCH, n_pages).astype(jnp.int32)
    return {
        "x_prompt": nrm(ks[1], (BATCH, SEQ, D_MODEL), 1.0),
        "x_sample": nrm(ks[2], (DEC_BATCH, DEC_SEQ, D_MODEL), 1.0),
        "cache_fox_k": nrm(ks[3], (DEPTH, n_pool, PAGE_SIZE, FX_H, FX_DH), 1.0),
        "cache_fox_v": nrm(ks[4], (DEPTH, n_pool, PAGE_SIZE, FX_H, FX_DH), 1.0),
        "cache_fox_logf": jax.nn.log_sigmoid(4.0 + nrm(ks[5], (DEPTH, n_pool, PAGE_SIZE, FX_H), 0.5)),
        "page_table": page_table,
        "state_hgrn": nrm(ks[6], (DEPTH, DEC_BATCH, HG_H, HG_DK, HG_DV), 0.5),
        "state_rwkv": nrm(ks[7], (DEPTH, DEC_BATCH, RW_H, RW_N, RW_N), 0.5),
        "state_rwkv_shift": nrm(ks[8], (DEPTH, DEC_BATCH, C_RWKV), 1.0),
        "norm_mix": 1.0 + nrm(ks[9], (DEPTH, D_MODEL), 0.02),
        "w_in": nrm(ks[10], (DEPTH, D_MODEL, C_IN), D_MODEL ** -0.5),
        "w_out": nrm(ks[11], (DEPTH, D_MIX, D_MODEL), 0.5 * D_MIX ** -0.5),
        "hgrn_lb": nrm(ks[12], (DEPTH, HG_WIDTH), 0.1),
        "hgrn_norm": 1.0 + nrm(ks[13], (DEPTH, HG_DV), 0.02),
        "rwkv_mu": jax.random.uniform(ks[14], (DEPTH, C_RWKV), jnp.float32),
        "rwkv_w0": jax.random.uniform(ks[15], (DEPTH, RW_WIDTH), jnp.float32, minval=-6.0, maxval=0.0),
        "rwkv_w2": nrm(ks[16], (DEPTH, LORA_W, RW_WIDTH), 0.1 * LORA_W ** -0.5),
        "rwkv_a0": nrm(ks[17], (DEPTH, RW_WIDTH), 0.1),
        "rwkv_a2": nrm(ks[18], (DEPTH, LORA_A, RW_WIDTH), LORA_A ** -0.5),
        "rwkv_g2": nrm(ks[19], (DEPTH, LORA_G, RW_WIDTH), LORA_G ** -0.5),
        "rwkv_kk": 1.0 + nrm(ks[20], (DEPTH, RW_WIDTH), 0.1),
        "rwkv_ka": 1.0 + nrm(ks[21], (DEPTH, RW_WIDTH), 0.1),
        "rwkv_rk": nrm(ks[22], (DEPTH, RW_H, RW_N), 0.1),
        "rwkv_lnx_w": 1.0 + nrm(ks[23], (DEPTH, RW_WIDTH), 0.02),
        "rwkv_lnx_b": nrm(ks[24], (DEPTH, RW_WIDTH), 0.02),
        "fox_qnorm": 1.0 + nrm(ks[25], (DEPTH, FX_DH), 0.02),
        "fox_knorm": 1.0 + nrm(ks[26], (DEPTH, FX_DH), 0.02),
        "fox_fbias": 4.0 + nrm(ks[27], (DEPTH, FX_H), 0.5),
        "norm_ffn": 1.0 + nrm(ks[28], (DEPTH, D_MODEL), 0.02),
        "ffn_w1": nrm(ks[29], (N_DENSE, D_MODEL, D_FF), D_MODEL ** -0.5),
        "ffn_w3": nrm(ks[30], (N_DENSE, D_MODEL, D_FF), D_MODEL ** -0.5),
        "ffn_w2": nrm(ks[31], (N_DENSE, D_FF, D_MODEL), 0.5 * D_FF ** -0.5),
        "moe_router": nrm(ks[32], (N_MOE, D_MODEL, N_EXPERTS), D_MODEL ** -0.5),
        "moe_w1": nrm(ks[33], (N_MOE, N_EXPERTS, D_MODEL, D_FF_EXPERT), D_MODEL ** -0.5),
        "moe_w3": nrm(ks[34], (N_MOE, N_EXPERTS, D_MODEL, D_FF_EXPERT), D_MODEL ** -0.5),
        "moe_w2": nrm(ks[35], (N_MOE, N_EXPERTS, D_FF_EXPERT, D_MODEL), 0.5 * D_FF_EXPERT ** -0.5),
    }


def reference(x_prompt, x_sample, cache_fox_k, cache_fox_v, cache_fox_logf, page_table, state_hgrn,
              state_rwkv, state_rwkv_shift, norm_mix, w_in, w_out, hgrn_lb, hgrn_norm, rwkv_mu, rwkv_w0,
              rwkv_w2, rwkv_a0, rwkv_a2, rwkv_g2, rwkv_kk, rwkv_ka, rwkv_rk, rwkv_lnx_w, rwkv_lnx_b,
              fox_qnorm, fox_knorm, fox_fbias, norm_ffn, ffn_w1, ffn_w3, ffn_w2, moe_router, moe_w1,
              moe_w3, moe_w2):
    B, T = x_prompt.shape[0], x_prompt.shape[1]
    Bd = x_sample.shape[0]
    past_len = page_table.shape[1] * PAGE_SIZE
    dt = x_prompt.dtype
    lb_all = jnp.cumsum(jax.nn.softmax(hgrn_lb.astype(jnp.float32), axis=0), axis=0)
    lb_all = lb_all - lb_all[:1]
    hg0_p = jnp.zeros((B, HG_H, HG_DK, HG_DV), state_hgrn.dtype)
    rw0_p = jnp.zeros((B, RW_H, RW_N, RW_N), state_rwkv.dtype)
    sh0_p = jnp.zeros((B, C_RWKV), state_rwkv_shift.dtype)
    pk0 = jnp.zeros((B, 0, FX_H, FX_DH), cache_fox_k.dtype)
    pv0 = jnp.zeros((B, 0, FX_H, FX_DH), cache_fox_v.dtype)
    plf0 = jnp.zeros((B, 0, FX_H), cache_fox_logf.dtype)
    hp, hs = x_prompt, x_sample
    kp, vp, lfp, hgp, rwp, shp = [], [], [], [], [], []
    ksm, vsm, lfs, hgs, rws, shs = [], [], [], [], [], []
    for l in range(DEPTH):
        fi = l // 2
        lw = {
            'norm_mix': norm_mix[l], 'w_in': w_in[l], 'w_out': w_out[l], 'lb': lb_all[l].astype(dt),
            'hgrn_norm': hgrn_norm[l], 'rwkv_mu': rwkv_mu[l], 'rwkv_w0': rwkv_w0[l], 'rwkv_w2': rwkv_w2[l],
            'rwkv_a0': rwkv_a0[l], 'rwkv_a2': rwkv_a2[l], 'rwkv_g2': rwkv_g2[l], 'rwkv_kk': rwkv_kk[l],
            'rwkv_ka': rwkv_ka[l], 'rwkv_rk': rwkv_rk[l], 'rwkv_lnx_w': rwkv_lnx_w[l],
            'rwkv_lnx_b': rwkv_lnx_b[l], 'fox_qnorm': fox_qnorm[l], 'fox_knorm': fox_knorm[l],
            'fox_fbias': fox_fbias[l], 'norm_ffn': norm_ffn[l],
            'ffn_w1': ffn_w1[fi] if l % 2 == 0 else None, 'ffn_w3': ffn_w3[fi] if l % 2 == 0 else None,
            'ffn_w2': ffn_w2[fi] if l % 2 == 0 else None,
            'moe_router': moe_router[fi] if l % 2 == 1 else None, 'moe_w1': moe_w1[fi] if l % 2 == 1 else None,
            'moe_w3': moe_w3[fi] if l % 2 == 1 else None, 'moe_w2': moe_w2[fi] if l % 2 == 1 else None,
        }
        hp, k_n, v_n, lf_n, s_h, s_r, s_s = layer(hp, lw, l, hg0_p, rw0_p, sh0_p, pk0, pv0, plf0)
        kp.append(k_n); vp.append(v_n); lfp.append(lf_n); hgp.append(s_h); rwp.append(s_r); shp.append(s_s)
        past_k = cache_fox_k[l][page_table].reshape(Bd, past_len, FX_H, FX_DH)
        past_v = cache_fox_v[l][page_table].reshape(Bd, past_len, FX_H, FX_DH)
        past_lf = cache_fox_logf[l][page_table].reshape(Bd, past_len, FX_H)
        hs, k_n, v_n, lf_n, s_h, s_r, s_s = layer(hs, lw, l, state_hgrn[l], state_rwkv[l], state_rwkv_shift[l],
                                                  past_k, past_v, past_lf)
        ksm.append(k_n); vsm.append(v_n); lfs.append(lf_n); hgs.append(s_h); rws.append(s_r); shs.append(s_s)
    return (hp, hs,
            jnp.stack(kp), jnp.stack(vp), jnp.stack(lfp), jnp.stack(hgp), jnp.stack(rwp), jnp.stack(shp),
            jnp.stack(ksm), jnp.stack(vsm), jnp.stack(lfs), jnp.stack(hgs), jnp.stack(rws), jnp.stack(shs))
```

```python
import functools
import math

import jax
import jax.numpy as jnp
from jax import lax
from jax.experimental import pallas as pl
from jax.experimental.pallas import tpu as pltpu

f32 = jnp.float32
bf16 = jnp.bfloat16
i32 = jnp.int32

D_MODEL = 2048
DEPTH = 4
PAGE = 128
HG_H, HG_D = 4, 128
HG_W = HG_H * HG_D
RW_H, RW_N = 8, 64
RW_W = RW_H * RW_N
LORA_W, LORA_A, LORA_G = 64, 64, 128
C_RWKV = 3 * RW_W + LORA_W + LORA_A + LORA_G
FX_H, FX_DH = 8, 128
FX_W = FX_H * FX_DH
N_EXPERTS = 8
EPS = 1e-6
RW_GN_EPS = 64e-5
CB_HG = 0
CB_RW = (4 * HG_W) // 128
CB_FQ = CB_RW + C_RWKV // 128
CB_FK = CB_FQ + FX_H
CB_FV = CB_FK + FX_H
N_PROJ = (CB_FV + FX_H) * 128

LANES = 128
VMEM_LIMIT_BYTES = 56 * 1024 * 1024

NT = (((1,), (1,)), ((), ()))
TN = (((0,), (0,)), ((), ()))


def _cparams(sem):
    return pltpu.CompilerParams(dimension_semantics=sem, vmem_limit_bytes=VMEM_LIMIT_BYTES)


def _split(x, n):
    parts, r = [], x
    for i in range(n):
        p = r.astype(bf16)
        parts.append(p)
        if i + 1 < n:
            r = r - p.astype(f32)
    return parts


def _dotx(a, b, na, nb, dims=None):
    ap = _split(a, na) if a.dtype != bf16 else [a]
    bp = _split(b, nb) if b.dtype != bf16 else [b]
    n = max(len(ap), len(bp))
    acc = None
    for i, x in enumerate(ap):
        for j, y in enumerate(bp):
            if i + j < n:
                t = (jnp.dot(x, y, preferred_element_type=f32) if dims is None
                     else lax.dot_general(x, y, dims, preferred_element_type=f32))
                acc = t if acc is None else acc + t
    return acc


def _bdot(a, b, dims=None):
    return _dotx(a.astype(bf16), b.astype(bf16), 1, 1, dims)


def _iota(shape, axis):
    return lax.broadcasted_iota(i32, shape, axis)


def _route(lg):
    lane = _iota(lg.shape, 1)
    l1 = jnp.where(lane < N_EXPERTS, lg, -jnp.inf)
    m1 = jnp.max(l1, axis=-1, keepdims=True)
    i1 = jnp.min(jnp.where(l1 == m1, lane, LANES), axis=-1, keepdims=True)
    l2 = jnp.where(lane == i1, -jnp.inf, l1)
    m2 = jnp.max(l2, axis=-1, keepdims=True)
    i2 = jnp.min(jnp.where(l2 == m2, lane, LANES), axis=-1, keepdims=True)
    e = jnp.exp(m2 - m1)
    den = 1.0 + e
    g1 = 1.0 / den
    g2 = e / den
    return jnp.where(lane == 0, i1.astype(f32),
                     jnp.where(lane == 1, i2.astype(f32),
                               jnp.where(lane == 2, g1, jnp.where(lane == 3, g2, 0.0))))


def _norm_kernel(*refs, has_small, route):
    if has_small:
        x_ref, g_ref, ws_ref, u_ref, s_ref = refs
    else:
        x_ref, g_ref, u_ref = refs
    x = x_ref[...]
    ms = jnp.mean(x * x, axis=-1, keepdims=True)
    u = x * lax.rsqrt(ms + EPS) * g_ref[...]
    u_ref[...] = u.astype(bf16)
    if has_small:
        s = _dotx(u, ws_ref[...], 2, 2)
        s_ref[...] = _route(s) if route else s


def _rmsnorm(x, g, w_small=None, route=False, tm=688):
    M, D = x.shape
    has_small = w_small is not None
    in_specs = [pl.BlockSpec((tm, D), lambda m: (m, 0)), pl.BlockSpec((1, D), lambda m: (0, 0))]
    args = [x, g.reshape(1, D)]
    out_shape = [jax.ShapeDtypeStruct((M, D), bf16)]
    out_specs = [pl.BlockSpec((tm, D), lambda m: (m, 0))]
    if has_small:
        ws = jnp.zeros((D, LANES), f32).at[:, :w_small.shape[1]].set(w_small)
        in_specs.append(pl.BlockSpec((D, LANES), lambda m: (0, 0)))
        args.append(ws)
        out_shape.append(jax.ShapeDtypeStruct((M, LANES), f32))
        out_specs.append(pl.BlockSpec((tm, LANES), lambda m: (m, 0)))
    outs = pl.pallas_call(
        functools.partial(_norm_kernel, has_small=has_small, route=route),
        grid=(M // tm,), in_specs=in_specs, out_specs=out_specs, out_shape=out_shape,
        compiler_params=_cparams(("parallel",)), name="rmsnorm")(*args)
    return outs if has_small else outs[0]


def _mm_kernel(*refs, glu, has_res):
    it = iter(refs)
    a_ref = next(it)
    w_ref = next(it)
    w3_ref = next(it) if glu else None
    r_ref = next(it) if has_res else None
    o_ref = next(it)
    wb_ref = next(it)
    wb3_ref = next(it) if glu else None

    @pl.when(pl.program_id(1) == 0)
    def _():
        wb_ref[...] = w_ref[...].astype(bf16)
        if glu:
            wb3_ref[...] = w3_ref[...].astype(bf16)

    a = a_ref[...]
    y = jnp.dot(a, wb_ref[...], preferred_element_type=f32)
    if glu:
        y = jax.nn.silu(y) * jnp.dot(a, wb3_ref[...], preferred_element_type=f32)
    if has_res:
        y = r_ref[...] + y
    o_ref[...] = y.astype(o_ref.dtype)


def _matmul(a, w, lidx, n_out, tn, tm, w3=None, res=None, out_dtype=f32, name="matmul"):
    M, K = a.shape
    glu = w3 is not None
    has_res = res is not None
    w_spec = pl.BlockSpec((None, K, tn), lambda n, m: (lidx, 0, n))
    in_specs = [pl.BlockSpec((tm, K), lambda n, m: (m, 0)), w_spec]
    args = [a, w]
    scratch = [pltpu.VMEM((K, tn), bf16)]
    if glu:
        in_specs.append(w_spec)
        args.append(w3)
        scratch.append(pltpu.VMEM((K, tn), bf16))
    if has_res:
        in_specs.append(pl.BlockSpec((tm, tn), lambda n, m: (m, n)))
        args.append(res)
    return pl.pallas_call(
        functools.partial(_mm_kernel, glu=glu, has_res=has_res),
        grid=(pl.cdiv(n_out, tn), M // tm), in_specs=in_specs,
        out_specs=pl.BlockSpec((tm, tn), lambda n, m: (m, n)),
        out_shape=jax.ShapeDtypeStruct((M, n_out), out_dtype), scratch_shapes=scratch,
        compiler_params=_cparams(("arbitrary", "arbitrary")), name=name)(*args)


def _hgrn_kernel(q_ref, f_ref, i_ref, g_ref, lb_ref, nw_ref, s0_ref, o_ref, s_out_ref,
                 st_s, qh_s, kh_s, gr_s, o_s, *, c, nt):
    t = pl.program_id(2)
    tb = q_ref.shape[0]
    lc = int(math.log2(c))

    @pl.when(t == 0)
    def _():
        st_s[...] = s0_ref[...].T

    lb = lb_ref[...]
    fl = f_ref[...]
    lf = jnp.logaddexp(jnp.log(lb), jnp.log1p(-lb) + jax.nn.log_sigmoid(fl))
    kh_s[...] = (1.0 - lb) * jax.nn.sigmoid(-fl)
    qh_s[...] = jax.nn.silu(q_ref[...])
    row = _iota((tb, tb), 0)
    col = _iota((tb, tb), 1)
    same = jnp.right_shift(row, lc) == jnp.right_shift(col, lc)
    bd = jnp.where(same, jnp.where(col <= row, 1.0, 0.0), 0.0).astype(bf16)
    gr_s[...] = _dotx(bd, lf, 1, 3)
    ones = jnp.ones((HG_D, HG_D), bf16)
    rows = _iota((c, HG_D), 0)

    def body(j, carry):
        r0 = pl.multiple_of(j * c, c)
        qs = qh_s[pl.ds(r0, c), :]
        ks = kh_s[pl.ds(r0, c), :]
        gs = gr_s[pl.ds(r0, c), :]
        vs = i_ref[pl.ds(r0, c), :]
        glast = gr_s[pl.ds(r0 + (c - 1), 1), :]
        qt = qs * jnp.exp(gs)
        kt = ks * jnp.exp(glast - gs)
        st = st_s[...]
        o = _bdot(qt, st, NT)
        xs = []
        for s in range(c):
            m = rows >= s
            e = jnp.exp(jnp.where(m, gs - gs[s:s + 1, :], 0.0))
            xs.append(jnp.where(m, qs * ks[s:s + 1, :] * e, 0.0))
        p = _dotx(jnp.concatenate(xs, axis=0), ones, 2, 1)
        for s in range(c):
            o = o + p[s * c:(s + 1) * c, :] * vs[s:s + 1, :]
        o_s[pl.ds(r0, c), :] = o
        st_s[...] = st * jnp.exp(glast) + _bdot(vs, kt, TN)
        return carry

    lax.fori_loop(0, tb // c, body, 0)
    o = o_s[...]
    y = o * lax.rsqrt(jnp.mean(o * o, axis=-1, keepdims=True) + EPS) * nw_ref[...]
    o_ref[...] = (y * jax.nn.silu(g_ref[...])).astype(o_ref.dtype)

    @pl.when(t == nt - 1)
    def _():
        s_out_ref[...] = st_s[...].T


def _hgrn(proj, row0, B, T, lb, nw, s0, c, tb, out_dtype):
    nt = T // tb
    rb0 = row0 // tb

    def col(k):
        return pl.BlockSpec((tb, HG_D), lambda b, h, t, k=k: (rb0 + b * nt + t, CB_HG + k * HG_H + h))

    st_spec = pl.BlockSpec((None, None, HG_D, HG_D), lambda b, h, t: (b, h, 0, 0))
    return pl.pallas_call(
        functools.partial(_hgrn_kernel, c=c, nt=nt),
        grid=(B, HG_H, nt),
        in_specs=[col(0), col(1), col(2), col(3),
                  pl.BlockSpec((None, 1, HG_D), lambda b, h, t: (h, 0, 0)),
                  pl.BlockSpec((1, HG_D), lambda b, h, t: (0, 0)),
                  st_spec],
        out_specs=[pl.BlockSpec((tb, HG_D), lambda b, h, t: (b * nt + t, h)), st_spec],
        out_shape=[jax.ShapeDtypeStruct((B * T, HG_W), out_dtype),
                   jax.ShapeDtypeStruct((B, HG_H, HG_D, HG_D), f32)],
        scratch_shapes=[pltpu.VMEM((HG_D, HG_D), f32)] + [pltpu.VMEM((tb, HG_D), f32)] * 4,
        compiler_params=_cparams(("parallel", "parallel", "arbitrary")), name="hgrn2",
    )(proj, proj, proj, proj, lb.reshape(HG_H, 1, HG_D), nw.reshape(1, HG_D), s0)


def _rwkv_kernel(*refs, c, nt):
    p_refs = refs[:7]
    (sh0_ref, s0_ref, mu_ref, w0_ref, w2_ref, a0_ref, a2_ref, g2_ref, kk_ref, ka_ref, rk_ref,
     lnw_ref, lnb_ref, bd1_ref, o_ref, s_out_ref, sh_out_ref, st_s, carry_s) = refs[7:]
    t = pl.program_id(1)

    @pl.when(t == 0)
    def _():
        st_s[...] = s0_ref[...]
        carry_s[...] = sh0_ref[...]

    p = jnp.concatenate([r[...] for r in p_refs], axis=1)
    prev = jnp.concatenate([carry_s[...], p[:-1]], axis=0)
    carry_s[...] = p[c - 1:c]
    sh_out_ref[...] = p[c - 1:c]
    xs = p + (prev - p) * mu_ref[...]
    o0 = 0
    r = xs[:, o0:o0 + RW_W]; o0 += RW_W
    wd = xs[:, o0:o0 + LORA_W]; o0 += LORA_W
    k = xs[:, o0:o0 + RW_W]; o0 += RW_W
    v = xs[:, o0:o0 + RW_W]; o0 += RW_W
    ad = xs[:, o0:o0 + LORA_A]; o0 += LORA_A
    gd = xs[:, o0:o0 + LORA_G]

    w = w0_ref[...] + _bdot(jnp.tanh(wd), w2_ref[...])
    lw = -jnp.exp(-jax.nn.softplus(-w) - 0.5)
    a = jax.nn.sigmoid(a0_ref[...] + _bdot(ad, a2_ref[...]))
    gate = _bdot(jax.nn.sigmoid(gd), g2_ref[...])
    bd1 = bd1_ref[...]
    kk = k * kk_ref[...]
    kk = kk / jnp.maximum(jnp.sqrt(_dotx(kk * kk, bd1, 2, 1)), 1e-12)
    k2 = k * (1.0 + (a - 1.0) * ka_ref[...])
    alpha = -kk
    beta = kk * a

    ri = _iota((c, c), 0)
    ci = _iota((c, c), 1)
    incl = ri >= ci
    strict = ri > ci
    cum = _dotx(jnp.where(incl, 1.0, 0.0).astype(bf16), lw, 1, 3)
    cum_c = cum[c - 1:c, :]
    e_pos = jnp.exp(cum)
    e_neg = jnp.exp(-cum)
    e_suf = jnp.exp(cum_c - cum)
    a_bar = alpha * jnp.exp(cum - lw)
    b_bar = beta * e_neg
    k_bar = k2 * e_neg
    r_bar = r * e_pos
    b_hat = beta * e_suf
    k_hat = k2 * e_suf

    head = jnp.right_shift(_iota((c, RW_W), 1), 6)
    masks = [head == h for h in range(RW_H)]
    stack = jnp.concatenate([jnp.where(m, a_bar, 0.0) for m in masks]
                            + [jnp.where(m, r_bar, 0.0) for m in masks], axis=0).astype(bf16)
    gb = lax.dot_general(stack, b_bar.astype(bf16), NT, preferred_element_type=f32)
    gk = lax.dot_general(stack, k_bar.astype(bf16), NT, preferred_element_type=f32)
    eye = jnp.where(ri == ci, 1.0, 0.0)
    vb = v.astype(bf16)
    a_til = jnp.zeros((c, RW_W), f32)
    u0 = jnp.zeros((c, RW_W), f32)
    y0 = jnp.zeros((c, RW_W), f32)
    prb = []
    for h in range(RW_H):
        lab = jnp.where(strict, gb[h * c:(h + 1) * c], 0.0)
        lak = jnp.where(strict, gk[h * c:(h + 1) * c], 0.0)
        prb.append(jnp.where(incl, gb[(RW_H + h) * c:(RW_H + h + 1) * c], 0.0))
        prk = jnp.where(incl, gk[(RW_H + h) * c:(RW_H + h + 1) * c], 0.0)
        inv = eye + lab
        pw = lab
        for _ in range(int(math.log2(c)) - 1):
            pw = _dotx(pw, pw, 2, 2)
            inv = inv + _dotx(inv, pw, 2, 2)
        t1 = _bdot(lak, vb)
        a_til = a_til + jnp.where(masks[h], _dotx(inv, a_bar, 2, 2), 0.0)
        u0 = u0 + jnp.where(masks[h], _dotx(inv, t1, 2, 2), 0.0)
        y0 = y0 + jnp.where(masks[h], _bdot(prk, vb), 0.0)

    st = st_s[...]
    stb = st.astype(bf16)
    u = _bdot(a_til, stb, NT) + u0
    y = _bdot(r_bar, stb, NT) + y0
    ub = u.astype(bf16)
    for h in range(RW_H):
        y = y + jnp.where(masks[h], _bdot(prb[h], ub), 0.0)
    upd = _bdot(jnp.concatenate([u, v], axis=0), jnp.concatenate([b_hat, k_hat], axis=0), TN)
    same = jnp.right_shift(_iota((RW_W, RW_W), 0), 6) == jnp.right_shift(_iota((RW_W, RW_W), 1), 6)
    st_s[...] = st * jnp.exp(cum_c) + jnp.where(same, upd, 0.0)

    inv_n = 1.0 / RW_N
    mean = _dotx(y, bd1, 2, 1) * inv_n
    d = y - mean
    var = _dotx(d * d, bd1, 2, 1) * inv_n
    yn = d * lax.rsqrt(var + RW_GN_EPS) * lnw_ref[...] + lnb_ref[...]
    bonus = _dotx(r * k2 * rk_ref[...], bd1, 2, 1) * v
    o_ref[...] = ((yn + bonus) * gate).astype(o_ref.dtype)

    @pl.when(t == nt - 1)
    def _():
        s_out_ref[...] = st_s[...]


def _rwkv(proj, row0, B, T, c, sh0, s0_bd, lw, out_dtype):
    nt = T // c
    rb0 = row0 // c
    pw = 256
    vec = lambda n: pl.BlockSpec((1, n), lambda b, t: (0, 0))
    mat = lambda r_, n: pl.BlockSpec((r_, n), lambda b, t: (0, 0))
    p_specs = [pl.BlockSpec((c, pw), lambda b, t, j=j: (rb0 + b * nt + t, (CB_RW * LANES) // pw + j))
               for j in range(C_RWKV // pw)]
    hh = jnp.arange(RW_W) // RW_N
    bd1 = (hh[:, None] == hh[None, :]).astype(bf16)
    return pl.pallas_call(
        functools.partial(_rwkv_kernel, c=c, nt=nt),
        grid=(B, nt),
        in_specs=p_specs + [
            pl.BlockSpec((None, 1, C_RWKV), lambda b, t: (b, 0, 0)),
            pl.BlockSpec((None, RW_W, RW_W), lambda b, t: (b, 0, 0)),
            vec(C_RWKV), vec(RW_W), mat(LORA_W, RW_W), vec(RW_W), mat(LORA_A, RW_W), mat(LORA_G, RW_W),
            vec(RW_W), vec(RW_W), vec(RW_W), vec(RW_W), vec(RW_W), mat(RW_W, RW_W)],
        out_specs=[pl.BlockSpec((c, RW_W), lambda b, t: (b * nt + t, 0)),
                   pl.BlockSpec((None, RW_W, RW_W), lambda b, t: (b, 0, 0)),
                   pl.BlockSpec((None, 1, C_RWKV), lambda b, t: (b, 0, 0))],
        out_shape=[jax.ShapeDtypeStruct((B * T, RW_W), out_dtype),
                   jax.ShapeDtypeStruct((B, RW_W, RW_W), f32),
                   jax.ShapeDtypeStruct((B, 1, C_RWKV), f32)],
        scratch_shapes=[pltpu.VMEM((RW_W, RW_W), f32), pltpu.VMEM((1, C_RWKV), f32)],
        compiler_params=_cparams(("parallel", "arbitrary")), name="rwkv7",
    )(*([proj] * (C_RWKV // pw)), sh0.reshape(B, 1, C_RWKV), s0_bd,
      lw['mu'].reshape(1, -1), lw['w0'].reshape(1, -1), lw['w2'], lw['a0'].reshape(1, -1), lw['a2'], lw['g2'],
      lw['kk'].reshape(1, -1), lw['ka'].reshape(1, -1), lw['rk'].reshape(1, -1),
      lw['lnx_w'].reshape(1, -1), lw['lnx_b'].reshape(1, -1), bd1)


def _to_blockdiag(s):
    B = s.shape[0]
    eye = jnp.eye(RW_H, dtype=s.dtype)
    return jnp.einsum('bhvk,hg->bhvgk', s, eye).reshape(B, RW_W, RW_W)


def _from_blockdiag(s):
    B = s.shape[0]
    s5 = s.reshape(B, RW_H, RW_N, RW_H, RW_N)
    idx = jnp.arange(RW_H)
    return s5[:, idx, :, idx, :].transpose(1, 0, 2, 3)


def _fox_prep_kernel(q_ref, k_ref, v_ref, qg_ref, kg_ref, qn_ref, kn_ref, kb_ref, vn_ref, vb_ref):
    def nrm(x, g):
        return x * lax.rsqrt(jnp.mean(x * x, axis=-1, keepdims=True) + EPS) * g
    qn_ref[...] = nrm(q_ref[...], qg_ref[...]).astype(qn_ref.dtype)
    kn = nrm(k_ref[...], kg_ref[...])
    kn_ref[...] = kn
    kb_ref[...] = kn.astype(kb_ref.dtype)
    v = v_ref[...]
    vn_ref[...] = v
    vb_ref[...] = v.astype(vb_ref.dtype)


def _fox_prep(proj, row0, rows, tt, qg, kg, lowp):
    rb0 = row0 // tt
    col = lambda cb: pl.BlockSpec((tt, FX_DH), lambda r, h, cb=cb: (rb0 + r, cb + h))
    out = pl.BlockSpec((tt, FX_DH), lambda r, h: (r, h))
    gsp = pl.BlockSpec((1, FX_DH), lambda r, h: (0, 0))
    sds = lambda dt: jax.ShapeDtypeStruct((rows, FX_W), dt)
    return pl.pallas_call(
        _fox_prep_kernel, grid=(rows // tt, FX_H),
        in_specs=[col(CB_FQ), col(CB_FK), col(CB_FV), gsp, gsp],
        out_specs=[out] * 5, out_shape=[sds(lowp), sds(f32), sds(lowp), sds(f32), sds(lowp)],
        compiler_params=_cparams(("parallel", "parallel")), name="fox_prep",
    )(proj, proj, proj, qg.reshape(1, FX_DH), kg.reshape(1, FX_DH))


def _fox_lf_kernel(ff_ref, b_ref, lf_ref, cum_ref, carry_s):
    tt = ff_ref.shape[0]

    @pl.when(pl.program_id(1) == 0)
    def _():
        carry_s[...] = jnp.zeros_like(carry_s)

    lf = jax.nn.log_sigmoid(ff_ref[...] + b_ref[...])
    tril = jnp.where(_iota((tt, tt), 0) >= _iota((tt, tt), 1), 1.0, 0.0).astype(bf16)
    cum = _dotx(tril, lf, 1, 3) + carry_s[...]
    carry_s[...] = cum[tt - 1:tt]
    lf_ref[...] = lf[:, :FX_H]
    cum_ref[...] = cum[:, :FX_H]


def _fox_lf(ffl, row0, B, T, tt, fbias):
    nt = T // tt
    rb0 = row0 // tt
    bias = jnp.zeros((1, LANES), f32).at[0, :FX_H].set(fbias)
    out = pl.BlockSpec((tt, FX_H), lambda b, t: (b * nt + t, 0))
    return pl.pallas_call(
        _fox_lf_kernel, grid=(B, nt),
        in_specs=[pl.BlockSpec((tt, LANES), lambda b, t: (rb0 + b * nt + t, 0)),
                  pl.BlockSpec((1, LANES), lambda b, t: (0, 0))],
        out_specs=[out, out], out_shape=[jax.ShapeDtypeStruct((B * T, FX_H), f32)] * 2,
        scratch_shapes=[pltpu.VMEM((1, LANES), f32)],
        compiler_params=_cparams(("parallel", "arbitrary")), name="fox_logf",
    )(ffl, bias)


def _fox_attn_kernel(q_ref, k_ref, v_ref, cq_ref, ck_ref, o_ref, m_s, l_s, acc_s, *, tq, tk, nk):
    qi = pl.program_id(2)
    ki = pl.program_id(3)

    @pl.when(ki == 0)
    def _():
        m_s[...] = jnp.full_like(m_s, -jnp.inf)
        l_s[...] = jnp.zeros_like(l_s)
        acc_s[...] = jnp.zeros_like(acc_s)

    @pl.when(ki * tk <= qi * tq + (tq - 1))
    def _():
        s = lax.dot_general(q_ref[...], k_ref[...], NT, preferred_element_type=f32) * (FX_DH ** -0.5)
        s = s + (cq_ref[...] - ck_ref[...])
        qpos = qi * tq + _iota((tq, tk), 0)
        kpos = ki * tk + _iota((tq, tk), 1)
        s = jnp.where(kpos <= qpos, s, -jnp.inf)
        m_old = m_s[...]
        m_new = jnp.maximum(m_old, jnp.max(s, axis=-1, keepdims=True))
        alpha = jnp.exp(m_old - m_new)
        p = jnp.exp(s - m_new)
        l_s[...] = alpha * l_s[...] + jnp.sum(p, axis=-1, keepdims=True)
        acc_s[...] = alpha * acc_s[...] + jnp.dot(p.astype(bf16), v_ref[...], preferred_element_type=f32)
        m_s[...] = m_new

    @pl.when(ki == nk - 1)
    def _():
        o_ref[...] = (acc_s[...] / l_s[...]).astype(o_ref.dtype)


def _fox_attn(qn, kb, vb, cum, B, T, tq, tk):
    nq, nk = T // tq, T // tk
    cq = cum.reshape(B, T, FX_H).transpose(0, 2, 1).reshape(B, FX_H, T, 1)
    ck = cq.reshape(B, FX_H, 1, T)
    last = lambda qi: (qi * tq + tq - 1) // tk
    kv = pl.BlockSpec((tk, FX_DH), lambda b, h, qi, ki: (b * nk + jnp.minimum(ki, last(qi)), h))
    return pl.pallas_call(
        functools.partial(_fox_attn_kernel, tq=tq, tk=tk, nk=nk),
        grid=(B, FX_H, nq, nk),
        in_specs=[pl.BlockSpec((tq, FX_DH), lambda b, h, qi, ki: (b * nq + qi, h)), kv, kv,
                  pl.BlockSpec((None, None, tq, 1), lambda b, h, qi, ki: (b, h, qi, 0)),
                  pl.BlockSpec((None, None, 1, tk), lambda b, h, qi, ki: (b, h, 0, jnp.minimum(ki, last(qi))))],
        out_specs=pl.BlockSpec((tq, FX_DH), lambda b, h, qi, ki: (b * nq + qi, h)),
        out_shape=jax.ShapeDtypeStruct((B * T, FX_W), bf16),
        scratch_shapes=[pltpu.VMEM((tq, 1), f32), pltpu.VMEM((tq, 1), f32), pltpu.VMEM((tq, FX_DH), f32)],
        compiler_params=_cparams(("parallel", "parallel", "parallel", "arbitrary")), name="fox_attn",
    )(qn, kb, vb, cq, ck)


def _rep_rows(x, td):
    return jnp.concatenate([jnp.broadcast_to(x[h:h + 1], (td, x.shape[1])) for h in range(x.shape[0])], axis=0)


def _fox_dec_kernel(pt_ref, q_ref, kn_ref, vn_ref, cqc_ref, cqr_ref, kc_ref, vc_ref, lft_ref, o_ref,
                    m_s, l_s, acc_s, suf_s, qe_s, *, td, n_steps):
    j = pl.program_id(1)
    nr = FX_H * td
    scale = FX_DH ** -0.5
    ltd = int(math.log2(td))
    hm = jnp.right_shift(_iota((nr, FX_W), 0), ltd) == jnp.right_shift(_iota((nr, FX_W), 1), 7)

    def update(s, vbf):
        m_old = m_s[...]
        m_new = jnp.maximum(m_old, jnp.max(s, axis=-1, keepdims=True))
        alpha = jnp.exp(m_old - m_new)
        p = jnp.exp(s - m_new)
        l_s[...] = alpha * l_s[...] + jnp.sum(p, axis=-1, keepdims=True)
        acc_s[...] = alpha * acc_s[...] + jnp.dot(p.astype(bf16), vbf, preferred_element_type=f32)
        m_s[...] = m_new

    @pl.when(j == 0)
    def _():
        q = q_ref[...]
        qe = jnp.where(hm, jnp.concatenate([q] * FX_H, axis=0), 0.0).astype(bf16)
        qe_s[...] = qe
        pad = jnp.zeros((PAGE - td, FX_W), f32)
        kp = jnp.concatenate([kn_ref[...], pad], axis=0).astype(bf16)
        vp = jnp.concatenate([vn_ref[...], pad], axis=0).astype(bf16)
        s = lax.dot_general(qe, kp, NT, preferred_element_type=f32) * scale
        s = s + (cqc_ref[...] - _rep_rows(cqr_ref[...], td))
        kpos = _iota((nr, PAGE), 1)
        qpos = jnp.bitwise_and(_iota((nr, PAGE), 0), td - 1)
        s = jnp.where(kpos <= qpos, s, -jnp.inf)
        m_s[...] = jnp.full_like(m_s, -jnp.inf)
        l_s[...] = jnp.zeros_like(l_s)
        acc_s[...] = jnp.zeros_like(acc_s)
        suf_s[...] = jnp.zeros_like(suf_s)
        update(s, vp)

    @pl.when(j > 0)
    def _():
        lft = lft_ref[...]
        later = jnp.where(_iota((PAGE, PAGE), 0) > _iota((PAGE, PAGE), 1), 1.0, 0.0).astype(bf16)
        suf = _dotx(lft, later, 3, 1) + suf_s[...]
        suf_s[...] = suf_s[...] + jnp.sum(lft, axis=-1, keepdims=True)
        s = lax.dot_general(qe_s[...], kc_ref[...].astype(bf16), NT, preferred_element_type=f32) * scale
        s = s + (cqc_ref[...] + _rep_rows(suf, td))
        update(s, vc_ref[...].astype(bf16))

    @pl.when(j == n_steps - 1)
    def _():
        a = jnp.where(hm, acc_s[...] / l_s[...], 0.0)
        o = a[0:td]
        for h in range(1, FX_H):
            o = o + a[h * td:(h + 1) * td]
        o_ref[...] = o.astype(o_ref.dtype)


def _fox_decode(qn, kn, vn, cnew, kcache, vcache, lft, page_table, page0, td):
    Bd, n_pages = page_table.shape
    n_steps = n_pages + 1
    nr = FX_H * td
    c3 = cnew.reshape(Bd, td, FX_H).transpose(0, 2, 1)
    cqc = c3.reshape(Bd, nr, 1)
    cqr = jnp.zeros((Bd, FX_H, PAGE), f32).at[:, :, :td].set(c3)
    page = lambda b, j, pt: page0 + pt[b, n_pages - jnp.maximum(j, 1)]
    tok = pl.BlockSpec((td, FX_W), lambda b, j, pt: (b, 0))
    kvs = pl.BlockSpec((None, PAGE, FX_W), lambda b, j, pt: (page(b, j, pt), 0, 0))
    grid_spec = pltpu.PrefetchScalarGridSpec(
        num_scalar_prefetch=1, grid=(Bd, n_steps),
        in_specs=[tok, tok, tok,
                  pl.BlockSpec((None, nr, 1), lambda b, j, pt: (b, 0, 0)),
                  pl.BlockSpec((None, FX_H, PAGE), lambda b, j, pt: (b, 0, 0)),
                  kvs, kvs,
                  pl.BlockSpec((None, FX_H, PAGE), lambda b, j, pt: (page(b, j, pt), 0, 0))],
        out_specs=pl.BlockSpec((td, FX_W), lambda b, j, pt: (b, 0)),
        scratch_shapes=[pltpu.VMEM((nr, 1), f32), pltpu.VMEM((nr, 1), f32), pltpu.VMEM((nr, FX_W), f32),
                        pltpu.VMEM((FX_H, PAGE), f32), pltpu.VMEM((nr, FX_W), bf16)])
    return pl.pallas_call(
        functools.partial(_fox_dec_kernel, td=td, n_steps=n_steps), grid_spec=grid_spec,
        out_shape=jax.ShapeDtypeStruct((Bd * td, FX_W), f32),
        compiler_params=_cparams(("parallel", "arbitrary")), name="fox_decode",
    )(page_table, qn, kn, vn, cqc, cqr, kcache, vcache, lft)


def _moe_plan_kernel(r_ref, pos_ref, te_ref, cs_s, *, rb, tm_e):
    M = r_ref.shape[0]
    nb = M // rb
    lane = _iota((rb, LANES), 1)
    strict = jnp.where(_iota((rb, rb), 0) > _iota((rb, rb), 1), 1.0, 0.0).astype(bf16)

    def onehots(blk):
        r = r_ref[blk * rb:(blk + 1) * rb, :]
        i1 = r[:, 0:1].astype(i32)
        i2 = r[:, 1:2].astype(i32)
        return lane == i1, lane == i2

    carry = jnp.zeros((1, LANES), f32)
    for blk in range(nb):
        o1, o2 = onehots(blk)
        oh = jnp.where(o1, 1.0, jnp.where(o2, 1.0, 0.0))
        cs_s[blk * rb:(blk + 1) * rb, :] = _dotx(strict, oh.astype(bf16), 1, 1) + carry
        carry = carry + jnp.sum(oh, axis=0, keepdims=True)
    padded = jnp.floor((carry + (tm_e - 1)) * (1.0 / tm_e)) * tm_e
    before = jnp.where(_iota((LANES, LANES), 0) < _iota((LANES, LANES), 1), 1.0, 0.0).astype(bf16)
    off = _dotx(jnp.broadcast_to(padded, (8, LANES)), before, 3, 1)[0:1]
    end = off + padded
    for blk in range(nb):
        o1, o2 = onehots(blk)
        base = off + cs_s[blk * rb:(blk + 1) * rb, :]
        p1 = jnp.sum(jnp.where(o1, base, 0.0), axis=-1, keepdims=True)
        p2 = jnp.sum(jnp.where(o2, base, 0.0), axis=-1, keepdims=True)
        pos_ref[blk * rb:(blk + 1) * rb, :] = jnp.where(lane == 0, p1, jnp.where(lane == 1, p2, 0.0)).astype(i32)
    l1 = _iota((1, LANES), 1)
    start = (l1 * tm_e).astype(f32)
    te = jnp.zeros((1, LANES), f32)
    for e in range(N_EXPERTS):
        end_e = jnp.sum(jnp.where(l1 == e, end, 0.0), axis=-1, keepdims=True)
        te = te + jnp.where(end_e <= start, 1.0, 0.0)
    total = jnp.sum(jnp.where(l1 == N_EXPERTS - 1, end, 0.0), axis=-1, keepdims=True)
    te = jnp.minimum(te, N_EXPERTS - 1.0)
    te_ref[...] = jnp.where(l1 == LANES - 1, total * (1.0 / tm_e), te).astype(i32)


def _moe_plan(route, tm_e, rb):
    M = route.shape[0]
    return pl.pallas_call(
        functools.partial(_moe_plan_kernel, rb=rb, tm_e=tm_e),
        out_shape=[jax.ShapeDtypeStruct((M, LANES), i32), jax.ShapeDtypeStruct((1, LANES), i32)],
        scratch_shapes=[pltpu.VMEM((M, LANES), f32)],
        compiler_params=pltpu.CompilerParams(vmem_limit_bytes=VMEM_LIMIT_BYTES), name="moe_plan")(route)


def _moe_scatter_kernel(pos_ref, u_ref, z_ref, xs_ref, sem, *, tb):
    del z_ref
    i0 = pl.program_id(0) * tb

    def copy(i, j):
        return pltpu.make_async_copy(u_ref.at[i0 + i], xs_ref.at[pos_ref[2 * (i0 + i) + j]], sem)

    def start(i, c):
        copy(i, 0).start()
        copy(i, 1).start()
        return c

    def wait(i, c):
        copy(i, 0).wait()
        copy(i, 1).wait()
        return c

    lax.fori_loop(0, tb, start, 0)
    lax.fori_loop(0, tb, wait, 0)


def _moe_scatter(u3, pos_flat, rows, tb):
    M = u3.shape[0]
    zeros = jnp.zeros((rows,) + u3.shape[1:], u3.dtype)
    grid_spec = pltpu.PrefetchScalarGridSpec(
        num_scalar_prefetch=1, grid=(M // tb,),
        in_specs=[pl.BlockSpec(memory_space=pl.ANY), pl.BlockSpec(memory_space=pl.ANY)],
        out_specs=pl.BlockSpec(memory_space=pl.ANY),
        scratch_shapes=[pltpu.SemaphoreType.DMA(())])
    return pl.pallas_call(
        functools.partial(_moe_scatter_kernel, tb=tb), grid_spec=grid_spec,
        out_shape=jax.ShapeDtypeStruct(zeros.shape, u3.dtype), input_output_aliases={2: 0},
        compiler_params=_cparams(("arbitrary",)), name="moe_scatter")(pos_flat, u3, zeros)


def _moe_mm_kernel(te_ref, *refs, glu):
    it = iter(refs)
    a_ref = next(it)
    w_ref = next(it)
    w3_ref = next(it) if glu else None
    o_ref = next(it)
    wb_ref = next(it)
    wb3_ref = next(it) if glu else None
    t = pl.program_id(1)
    e = te_ref[t]
    e_prev = te_ref[jnp.maximum(t - 1, 0)]

    @pl.when((t == 0) | (e != e_prev))
    def _():
        wb_ref[...] = w_ref[...].astype(bf16)
        if glu:
            wb3_ref[...] = w3_ref[...].astype(bf16)

    @pl.when(t < te_ref[LANES - 1])
    def _():
        a = a_ref[...]
        y = jnp.dot(a, wb_ref[...], preferred_element_type=f32)
        if glu:
            y = jax.nn.silu(y) * jnp.dot(a, wb3_ref[...], preferred_element_type=f32)
        o_ref[...] = y.astype(o_ref.dtype)

    @pl.when(t >= te_ref[LANES - 1])
    def _():
        o_ref[...] = jnp.zeros_like(o_ref)


def _moe_matmul(a, te, w, fi, tm_e, tn, w3=None, out_dtype=f32, name="moe_matmul"):
    R, K = a.shape
    N = w.shape[-1]
    glu = w3 is not None
    w_spec = pl.BlockSpec((None, None, K, tn), lambda n, t, te_: (fi, te_[t], 0, n))
    in_specs = [pl.BlockSpec((tm_e, K), lambda n, t, te_: (t, 0)), w_spec] + ([w_spec] if glu else [])
    args = [a, w] + ([w3] if glu else [])
    scratch = [pltpu.VMEM((K, tn), bf16)] * (2 if glu else 1)
    grid_spec = pltpu.PrefetchScalarGridSpec(
        num_scalar_prefetch=1, grid=(N // tn, R // tm_e), in_specs=in_specs,
        out_specs=pl.BlockSpec((tm_e, tn), lambda n, t, te_: (t, n)), scratch_shapes=scratch)
    return pl.pallas_call(
        functools.partial(_moe_mm_kernel, glu=glu), grid_spec=grid_spec,
        out_shape=jax.ShapeDtypeStruct((R, N), out_dtype),
        compiler_params=_cparams(("arbitrary", "arbitrary")), name=name)(te, *args)


def _moe_combine_kernel(pos_ref, h_ref, r_ref, y_ref, o_ref, buf0, buf1, sem, *, tb):
    i0 = pl.program_id(0) * tb
    bufs = (buf0, buf1)

    def copy(i, j):
        return pltpu.make_async_copy(y_ref.at[pos_ref[2 * (i0 + i) + j]], bufs[j].at[i], sem)

    def start(i, c):
        copy(i, 0).start()
        copy(i, 1).start()
        return c

    def wait(i, c):
        copy(i, 0).wait()
        copy(i, 1).wait()
        return c

    lax.fori_loop(0, tb, start, 0)
    lax.fori_loop(0, tb, wait, 0)
    r = r_ref[...]
    g1 = r[:, 2:3]
    g2 = r[:, 3:4]
    for s in range(h_ref.shape[1] // LANES):
        sl = slice(s * LANES, (s + 1) * LANES)
        o_ref[:, sl] = h_ref[:, sl] + (g1 * buf0[:, s, :] + g2 * buf1[:, s, :])


def _moe_combine(h, route, y3, pos_flat, tb):
    M, D = h.shape
    grid_spec = pltpu.PrefetchScalarGridSpec(
        num_scalar_prefetch=1, grid=(M // tb,),
        in_specs=[pl.BlockSpec((tb, D), lambda m, p: (m, 0)), pl.BlockSpec((tb, LANES), lambda m, p: (m, 0)),
                  pl.BlockSpec(memory_space=pl.ANY)],
        out_specs=pl.BlockSpec((tb, D), lambda m, p: (m, 0)),
        scratch_shapes=[pltpu.VMEM((tb,) + y3.shape[1:], f32), pltpu.VMEM((tb,) + y3.shape[1:], f32),
                        pltpu.SemaphoreType.DMA(())])
    return pl.pallas_call(
        functools.partial(_moe_combine_kernel, tb=tb), grid_spec=grid_spec,
        out_shape=jax.ShapeDtypeStruct((M, D), f32),
        compiler_params=_cparams(("arbitrary",)), name="moe_combine")(pos_flat, h, route, y3)


def _moe(h, u, route, w1, w3, w2, fi, tm, tm_e=256):
    M = h.shape[0]
    rows = (pl.cdiv(2 * M, tm_e) + N_EXPERTS) * tm_e
    pos, te = _moe_plan(route, tm_e, tm)
    pos_flat = pos[:, :2].reshape(-1)
    te = te.reshape(-1)
    D = h.shape[1]
    xs = _moe_scatter(u.reshape(M, D // LANES, LANES), pos_flat, rows, tm).reshape(rows, D)
    act = _moe_matmul(xs, te, w1, fi, tm_e, 1024, w3=w3, out_dtype=bf16, name="moe_up")
    y = _moe_matmul(act, te, w2, fi, tm_e, 512, name="moe_down")
    return _moe_combine(h, route, y.reshape(rows, D // LANES, LANES), pos_flat, tm // 2)


def kernel(x_prompt, x_sample, cache_fox_k, cache_fox_v, cache_fox_logf, page_table, state_hgrn, state_rwkv, state_rwkv_shift, norm_mix, w_in, w_out, hgrn_lb, hgrn_norm, rwkv_mu, rwkv_w0, rwkv_w2, rwkv_a0, rwkv_a2, rwkv_g2, rwkv_kk, rwkv_ka, rwkv_rk, rwkv_lnx_w, rwkv_lnx_b, fox_qnorm, fox_knorm, fox_fbias, norm_ffn, ffn_w1, ffn_w3, ffn_w2, moe_router, moe_w1, moe_w3, moe_w2):
    B, T, D = x_prompt.shape
    Bd, Td, _ = x_sample.shape
    depth = w_in.shape[0]
    n_pool = cache_fox_k.shape[1]
    Mp, Ms = B * T, Bd * Td
    M = Mp + Ms
    tm = 688 if M % 688 == 0 else M
    d_ff = ffn_w1.shape[-1]

    h = jnp.concatenate([x_prompt.reshape(Mp, D), x_sample.reshape(Ms, D)], axis=0)
    lb_all = jnp.cumsum(jax.nn.softmax(hgrn_lb.astype(f32), axis=0), axis=0)
    lb_all = lb_all - lb_all[:1]
    kcache = cache_fox_k.reshape(depth * n_pool, PAGE, FX_W)
    vcache = cache_fox_v.reshape(depth * n_pool, PAGE, FX_W)
    lft = cache_fox_logf.transpose(0, 1, 3, 2).reshape(depth * n_pool, FX_H, PAGE)
    zeros_hg = jnp.zeros((B, HG_H, HG_D, HG_D), f32)
    zeros_rw = jnp.zeros((B, RW_W, RW_W), f32)
    zeros_sh = jnp.zeros((B, C_RWKV), f32)
    c_p = 64 if T % 64 == 0 else T
    tb_p = 256 if T % 256 == 0 else T
    tt_p = 512 if T % 512 == 0 else T

    outs = [[] for _ in range(12)]
    for l in range(depth):
        fi = l // 2
        rw = dict(mu=rwkv_mu[l], w0=rwkv_w0[l], w2=rwkv_w2[l], a0=rwkv_a0[l], a2=rwkv_a2[l], g2=rwkv_g2[l],
                  kk=rwkv_kk[l], ka=rwkv_ka[l], rk=rwkv_rk[l], lnx_w=rwkv_lnx_w[l], lnx_b=rwkv_lnx_b[l])
        u, ffl = _rmsnorm(h, norm_mix[l], w_small=w_in[l][:, N_PROJ:], tm=tm)
        proj = _matmul(u, w_in, l, N_PROJ, 768, tm, name="mm_in")

        o_hg, s_hg = _hgrn(proj, 0, B, T, lb_all[l], hgrn_norm[l], zeros_hg, 16 if tb_p % 16 == 0 else tb_p, tb_p, bf16)
        o_rw, s_rw, sh = _rwkv(proj, 0, B, T, c_p, zeros_sh, zeros_rw, rw, bf16)
        qn, kn, kb, vn, vb = _fox_prep(proj, 0, Mp, min(1024, Mp), fox_qnorm[l], fox_knorm[l], bf16)
        lf, cum = _fox_lf(ffl, 0, B, T, tt_p, fox_fbias[l])
        o_fx = _fox_attn(qn, kb, vb, cum, B, T, tt_p, tt_p)
        for i, x in enumerate((kn.reshape(B, T, FX_H, FX_DH), vn.reshape(B, T, FX_H, FX_DH),
                               lf.reshape(B, T, FX_H), s_hg, _from_blockdiag(s_rw), sh.reshape(B, C_RWKV))):
            outs[i].append(x)

        o_hg_s, s_hg_s = _hgrn(proj, Mp, Bd, Td, lb_all[l], hgrn_norm[l], state_hgrn[l], Td, Td, f32)
        o_rw_s, s_rw_s, sh_s = _rwkv(proj, Mp, Bd, Td, Td, state_rwkv_shift[l], _to_blockdiag(state_rwkv[l]), rw, f32)
        qn_s, kn_s, _, vn_s, _ = _fox_prep(proj, Mp, Ms, Ms, fox_qnorm[l], fox_knorm[l], f32)
        lf_s, cnew = _fox_lf(ffl, Mp, Bd, Td, Td, fox_fbias[l])
        o_fx_s = _fox_decode(qn_s, kn_s, vn_s, cnew, kcache, vcache, lft, page_table, l * n_pool, Td)
        for i, x in enumerate((kn_s.reshape(Bd, Td, FX_H, FX_DH), vn_s.reshape(Bd, Td, FX_H, FX_DH),
                               lf_s.reshape(Bd, Td, FX_H), s_hg_s, _from_blockdiag(s_rw_s), sh_s.reshape(Bd, C_RWKV))):
            outs[6 + i].append(x)

        mix = jnp.concatenate([jnp.concatenate([o_hg, o_rw, o_fx], axis=1),
                               jnp.concatenate([o_hg_s, o_rw_s, o_fx_s], axis=1).astype(bf16)], axis=0)
        h = _matmul(mix, w_out, l, D, 512, tm, res=h, name="mm_out")
        if l % 2 == 0:
            u2 = _rmsnorm(h, norm_ffn[l], tm=tm)
            act = _matmul(u2, ffn_w1, fi, d_ff, 512, tm, w3=ffn_w3, out_dtype=bf16, name="ffn_up")
            h = _matmul(act, ffn_w2, fi, D, 512, tm, res=h, name="ffn_down")
        else:
            u2, route = _rmsnorm(h, norm_ffn[l], w_small=moe_router[fi], route=True, tm=tm)
            h = _moe(h, u2, route, moe_w1, moe_w3, moe_w2, fi, tm)

    st = [jnp.stack(o) for o in outs]
    return (h[:Mp].reshape(B, T, D), h[Mp:].reshape(Bd, Td, D), *st)
```

```python
import functools
import math

import jax
import jax.numpy as jnp
from jax import lax
from jax.experimental import pallas as pl
from jax.experimental.pallas import tpu as pltpu

f32 = jnp.float32
bf16 = jnp.bfloat16
i32 = jnp.int32

D_MODEL = 2048
DEPTH = 4
PAGE = 128
HG_H, HG_D = 4, 128
HG_W = HG_H * HG_D
RW_H, RW_N = 8, 64
RW_W = RW_H * RW_N
LORA_W, LORA_A, LORA_G = 64, 64, 128
C_RWKV = 3 * RW_W + LORA_W + LORA_A + LORA_G
FX_H, FX_DH = 8, 128
FX_W = FX_H * FX_DH
N_EXPERTS = 8
EPS = 1e-6
RW_GN_EPS = 64e-5
CB_HG = 0
CB_RW = (4 * HG_W) // 128
CB_FQ = CB_RW + C_RWKV // 128
CB_FK = CB_FQ + FX_H
CB_FV = CB_FK + FX_H
N_PROJ = (CB_FV + FX_H) * 128

LANES = 128
VMEM_LIMIT_BYTES = 56 * 1024 * 1024

NT = (((1,), (1,)), ((), ()))
TN = (((0,), (0,)), ((), ()))


def _cparams(sem):
    return pltpu.CompilerParams(dimension_semantics=sem, vmem_limit_bytes=VMEM_LIMIT_BYTES)


def _split(x, n):
    parts, r = [], x
    for i in range(n):
        p = r.astype(bf16)
        parts.append(p)
        if i + 1 < n:
            r = r - p.astype(f32)
    return parts


def _dotx(a, b, na, nb, dims=None):
    ap = _split(a, na) if a.dtype != bf16 else [a]
    bp = _split(b, nb) if b.dtype != bf16 else [b]
    n = max(len(ap), len(bp))
    acc = None
    for i, x in enumerate(ap):
        for j, y in enumerate(bp):
            if i + j < n:
                t = (jnp.dot(x, y, preferred_element_type=f32) if dims is None
                     else lax.dot_general(x, y, dims, preferred_element_type=f32))
                acc = t if acc is None else acc + t
    return acc


def _bdot(a, b, dims=None):
    return _dotx(a.astype(bf16), b.astype(bf16), 1, 1, dims)


def _iota(shape, axis):
    return lax.broadcasted_iota(i32, shape, axis)


def _route(lg):
    lane = _iota(lg.shape, 1)
    l1 = jnp.where(lane < N_EXPERTS, lg, -jnp.inf)
    m1 = jnp.max(l1, axis=-1, keepdims=True)
    i1 = jnp.min(jnp.where(l1 == m1, lane, LANES), axis=-1, keepdims=True)
    l2 = jnp.where(lane == i1, -jnp.inf, l1)
    m2 = jnp.max(l2, axis=-1, keepdims=True)
    i2 = jnp.min(jnp.where(l2 == m2, lane, LANES), axis=-1, keepdims=True)
    e = jnp.exp(m2 - m1)
    den = 1.0 + e
    g1 = 1.0 / den
    g2 = e / den
    return jnp.where(lane == 0, i1.astype(f32),
                     jnp.where(lane == 1, i2.astype(f32),
                               jnp.where(lane == 2, g1, jnp.where(lane == 3, g2, 0.0))))


def _norm_kernel(*refs, has_small, route):
    if has_small:
        x_ref, g_ref, ws_ref, u_ref, s_ref = refs
    else:
        x_ref, g_ref, u_ref = refs
    x = x_ref[...]
    ms = jnp.mean(x * x, axis=-1, keepdims=True)
    u = x * lax.rsqrt(ms + EPS) * g_ref[...]
    if len(u_ref.shape) == 3:
        for s in range(u_ref.shape[1]):
            u_ref[:, s, :] = u[:, s * LANES:(s + 1) * LANES]
    else:
        u_ref[...] = u.astype(bf16)
    if has_small:
        s = _dotx(u, ws_ref[...], 2, 2)
        s_ref[...] = _route(s) if route else s


def _rmsnorm(x, g, w_small=None, route=False, tm=688, slabs=False):
    M, D = x.shape
    has_small = w_small is not None
    in_specs = [pl.BlockSpec((tm, D), lambda m: (m, 0)), pl.BlockSpec((1, D), lambda m: (0, 0))]
    args = [x, g.reshape(1, D)]
    if slabs:
        out_shape = [jax.ShapeDtypeStruct((M, D // LANES, LANES), f32)]
        out_specs = [pl.BlockSpec((tm, D // LANES, LANES), lambda m: (m, 0, 0))]
    else:
        out_shape = [jax.ShapeDtypeStruct((M, D), bf16)]
        out_specs = [pl.BlockSpec((tm, D), lambda m: (m, 0))]
    if has_small:
        ws = jnp.zeros((D, LANES), f32).at[:, :w_small.shape[1]].set(w_small)
        in_specs.append(pl.BlockSpec((D, LANES), lambda m: (0, 0)))
        args.append(ws)
        out_shape.append(jax.ShapeDtypeStruct((M, LANES), f32))
        out_specs.append(pl.BlockSpec((tm, LANES), lambda m: (m, 0)))
    outs = pl.pallas_call(
        functools.partial(_norm_kernel, has_small=has_small, route=route),
        grid=(M // tm,), in_specs=in_specs, out_specs=out_specs, out_shape=out_shape,
        compiler_params=_cparams(("parallel",)), name="rmsnorm")(*args)
    return outs if has_small else outs[0]


def _mm_kernel(*refs, glu, has_res):
    it = iter(refs)
    a_ref = next(it)
    w_ref = next(it)
    w3_ref = next(it) if glu else None
    r_ref = next(it) if has_res else None
    o_ref = next(it)
    wb_ref = next(it)
    wb3_ref = next(it) if glu else None

    @pl.when(pl.program_id(1) == 0)
    def _():
        wb_ref[...] = w_ref[...].astype(bf16)
        if glu:
            wb3_ref[...] = w3_ref[...].astype(bf16)

    a = a_ref[...]
    y = jnp.dot(a, wb_ref[...], preferred_element_type=f32)
    if glu:
        y = jax.nn.silu(y) * jnp.dot(a, wb3_ref[...], preferred_element_type=f32)
    if has_res:
        y = r_ref[...] + y
    o_ref[...] = y.astype(o_ref.dtype)


def _matmul(a, w, lidx, n_out, tn, tm, w3=None, res=None, out_dtype=f32, name="matmul"):
    M, K = a.shape
    glu = w3 is not None
    has_res = res is not None
    w_spec = pl.BlockSpec((None, K, tn), lambda n, m: (lidx, 0, n))
    in_specs = [pl.BlockSpec((tm, K), lambda n, m: (m, 0)), w_spec]
    args = [a, w]
    scratch = [pltpu.VMEM((K, tn), bf16)]
    if glu:
        in_specs.append(w_spec)
        args.append(w3)
        scratch.append(pltpu.VMEM((K, tn), bf16))
    if has_res:
        in_specs.append(pl.BlockSpec((tm, tn), lambda n, m: (m, n)))
        args.append(res)
    return pl.pallas_call(
        functools.partial(_mm_kernel, glu=glu, has_res=has_res),
        grid=(pl.cdiv(n_out, tn), M // tm), in_specs=in_specs,
        out_specs=pl.BlockSpec((tm, tn), lambda n, m: (m, n)),
        out_shape=jax.ShapeDtypeStruct((M, n_out), out_dtype), scratch_shapes=scratch,
        compiler_params=_cparams(("arbitrary", "arbitrary")), name=name)(*args)


def _hgrn_kernel(q_ref, f_ref, i_ref, g_ref, lb_ref, nw_ref, s0_ref, o_ref, s_out_ref,
                 st_s, qh_s, kh_s, gr_s, o_s, *, c, nt):
    t = pl.program_id(2)
    tb = q_ref.shape[0]
    lc = int(math.log2(c))

    @pl.when(t == 0)
    def _():
        st_s[...] = s0_ref[...].T

    lb = lb_ref[...]
    fl = f_ref[...]
    lf = jnp.logaddexp(jnp.log(lb), jnp.log1p(-lb) + jax.nn.log_sigmoid(fl))
    kh_s[...] = (1.0 - lb) * jax.nn.sigmoid(-fl)
    qh_s[...] = jax.nn.silu(q_ref[...])
    row = _iota((tb, tb), 0)
    col = _iota((tb, tb), 1)
    same = jnp.right_shift(row, lc) == jnp.right_shift(col, lc)
    bd = jnp.where(same, jnp.where(col <= row, 1.0, 0.0), 0.0).astype(bf16)
    gr_s[...] = _dotx(bd, lf, 1, 3)
    ones = jnp.ones((HG_D, HG_D), bf16)
    rows = _iota((c, HG_D), 0)

    nsb = tb // c
    incs, decs = [], []
    for j in range(nsb):
        r0 = j * c
        qs = qh_s[pl.ds(r0, c), :]
        ks = kh_s[pl.ds(r0, c), :]
        gs = gr_s[pl.ds(r0, c), :]
        vs = i_ref[pl.ds(r0, c), :]
        glast = gr_s[pl.ds(r0 + (c - 1), 1), :]
        xs = []
        for s in range(c):
            m = rows >= s
            e = jnp.exp(jnp.where(m, gs - gs[s:s + 1, :], 0.0))
            xs.append(jnp.where(m, qs * ks[s:s + 1, :] * e, 0.0))
        p = _dotx(jnp.concatenate(xs, axis=0), ones, 2, 1)
        o = p[0:c, :] * vs[0:1, :]
        for s in range(1, c):
            o = o + p[s * c:(s + 1) * c, :] * vs[s:s + 1, :]
        o_s[pl.ds(r0, c), :] = o
        incs.append(_bdot(vs, ks * jnp.exp(glast - gs), TN))
        decs.append(jnp.exp(glast))
    st = st_s[...]
    for j in range(nsb):
        r0 = j * c
        qt = qh_s[pl.ds(r0, c), :] * jnp.exp(gr_s[pl.ds(r0, c), :])
        o_s[pl.ds(r0, c), :] = o_s[pl.ds(r0, c), :] + _bdot(qt, st, NT)
        st = st * decs[j] + incs[j]

    st_s[...] = st
    o = o_s[...]
    y = o * lax.rsqrt(jnp.mean(o * o, axis=-1, keepdims=True) + EPS) * nw_ref[...]
    o_ref[...] = (y * jax.nn.silu(g_ref[...])).astype(o_ref.dtype)

    @pl.when(t == nt - 1)
    def _():
        s_out_ref[...] = st_s[...].T


def _hgrn(proj, row0, B, T, lb, nw, s0, c, tb, out_dtype):
    nt = T // tb
    rb0 = row0 // tb

    def col(k):
        return pl.BlockSpec((tb, HG_D), lambda b, h, t, k=k: (rb0 + b * nt + t, CB_HG + k * HG_H + h))

    st_spec = pl.BlockSpec((None, None, HG_D, HG_D), lambda b, h, t: (b, h, 0, 0))
    return pl.pallas_call(
        functools.partial(_hgrn_kernel, c=c, nt=nt),
        grid=(B, HG_H, nt),
        in_specs=[col(0), col(1), col(2), col(3),
                  pl.BlockSpec((None, 1, HG_D), lambda b, h, t: (h, 0, 0)),
                  pl.BlockSpec((1, HG_D), lambda b, h, t: (0, 0)),
                  st_spec],
        out_specs=[pl.BlockSpec((tb, HG_D), lambda b, h, t: (b * nt + t, h)), st_spec],
        out_shape=[jax.ShapeDtypeStruct((B * T, HG_W), out_dtype),
                   jax.ShapeDtypeStruct((B, HG_H, HG_D, HG_D), f32)],
        scratch_shapes=[pltpu.VMEM((HG_D, HG_D), f32)] + [pltpu.VMEM((tb, HG_D), f32)] * 4,
        compiler_params=_cparams(("parallel", "parallel", "arbitrary")), name="hgrn2",
    )(proj, proj, proj, proj, lb.reshape(HG_H, 1, HG_D), nw.reshape(1, HG_D), s0)


def _rwkv_kernel(*refs, c, nt, bb, parts):
    npr = C_RWKV // 256
    p_refs = refs[:npr * bb]
    (sh0_ref, s0_ref, mu_ref, w0_ref, w2_ref, a0_ref, a2_ref, g2_ref, kk_ref, ka_ref, rk_ref,
     lnw_ref, lnb_ref, bd1_ref) = refs[npr * bb:npr * bb + 14]
    o_refs = refs[npr * bb + 14:npr * bb + 14 + bb]
    s_out_ref, sh_out_ref, st_s, carry_s = refs[npr * bb + 14 + bb:]
    t = pl.program_id(1)

    @pl.when(t == 0)
    def _():
        st_s[...] = s0_ref[...]
        carry_s[...] = sh0_ref[...]

    B_ = range(bb)
    H_ = range(RW_H)
    ps = [jnp.concatenate([r[...] for r in p_refs[bi * npr:(bi + 1) * npr]], axis=1) for bi in B_]
    xs = []
    for bi in B_:
        p = ps[bi]
        prev = jnp.concatenate([carry_s[bi], p[:-1]], axis=0)
        carry_s[bi] = p[c - 1:c]
        sh_out_ref[bi] = p[c - 1:c]
        xs.append(p + (prev - p) * mu_ref[...])
    o1 = RW_W
    o2 = o1 + LORA_W
    o3 = o2 + RW_W
    o4 = o3 + RW_W
    o5 = o4 + LORA_A
    r = [x[:, :o1] for x in xs]
    k = [x[:, o2:o3] for x in xs]
    v = [x[:, o3:o4] for x in xs]
    wl = [_bdot(jnp.tanh(x[:, o1:o2]), w2_ref[...]) for x in xs]
    al = [_bdot(x[:, o4:o5], a2_ref[...]) for x in xs]
    gate = [_bdot(jax.nn.sigmoid(x[:, o5:]), g2_ref[...]) for x in xs]
    lw = [-jnp.exp(-jax.nn.softplus(-(w0_ref[...] + w)) - 0.5) for w in wl]
    a = [jax.nn.sigmoid(a0_ref[...] + x) for x in al]
    bd1 = bd1_ref[...]
    kk = [x * kk_ref[...] for x in k]
    ssq = [_dotx(x * x, bd1, 2, 1) for x in kk]
    ri = _iota((c, c), 0)
    ci = _iota((c, c), 1)
    incl = ri >= ci
    strict = ri > ci
    tril = jnp.where(incl, 1.0, 0.0).astype(bf16)
    cum = [_dotx(tril, x, 1, 3) for x in lw]
    kk = [x / jnp.maximum(jnp.sqrt(s), 1e-12) for x, s in zip(kk, ssq)]
    k2 = [k[bi] * (1.0 + (a[bi] - 1.0) * ka_ref[...]) for bi in B_]
    beta = [kk[bi] * a[bi] for bi in B_]
    cum_c = [x[c - 1:c, :] for x in cum]
    e_neg = [jnp.exp(-x) for x in cum]
    e_suf = [jnp.exp(cum_c[bi] - cum[bi]) for bi in B_]
    a_bar = [-kk[bi] * jnp.exp(cum[bi] - lw[bi]) for bi in B_]
    r_bar = [r[bi] * jnp.exp(cum[bi]) for bi in B_]
    b_bar = [beta[bi] * e_neg[bi] for bi in B_]
    k_bar = [k2[bi] * e_neg[bi] for bi in B_]
    b_hat = [beta[bi] * e_suf[bi] for bi in B_]
    k_hat = [k2[bi] * e_suf[bi] for bi in B_]

    head = jnp.right_shift(_iota((c, RW_W), 1), 6)
    masks = [head == h for h in H_]
    stack = [jnp.concatenate([jnp.where(m, a_bar[bi], 0.0) for m in masks]
                             + [jnp.where(m, r_bar[bi], 0.0) for m in masks], axis=0).astype(bf16) for bi in B_]
    gb = [lax.dot_general(stack[bi], b_bar[bi].astype(bf16), NT, preferred_element_type=f32) for bi in B_]
    gk = [lax.dot_general(stack[bi], k_bar[bi].astype(bf16), NT, preferred_element_type=f32) for bi in B_]
    eye = jnp.where(ri == ci, 1.0, 0.0)
    vb = [x.astype(bf16) for x in v]
    bh = [(bi, h) for bi in B_ for h in H_]
    blk = lambda g, i: g[i * c:(i + 1) * c]
    lab = {(bi, h): jnp.where(strict, blk(gb[bi], h), 0.0) for bi, h in bh}
    lak = {(bi, h): jnp.where(strict, blk(gk[bi], h), 0.0) for bi, h in bh}
    prb = {(bi, h): jnp.where(incl, blk(gb[bi], RW_H + h), 0.0) for bi, h in bh}
    prk = {(bi, h): jnp.where(incl, blk(gk[bi], RW_H + h), 0.0) for bi, h in bh}
    t1 = {q: _bdot(lak[q], vb[q[0]]) for q in bh}
    y0p = {q: _bdot(prk[q], vb[q[0]]) for q in bh}
    inv = {q: eye + lab[q] for q in bh}
    pw = lab
    for _ in range(int(math.log2(c)) - 1):
        pw = {q: _dotx(pw[q], pw[q], parts, parts) for q in bh}
        inv = {q: inv[q] + _dotx(inv[q], pw[q], parts, parts) for q in bh}
    atp = {q: _dotx(inv[q], a_bar[q[0]], parts, parts) for q in bh}
    u0p = {q: _dotx(inv[q], t1[q], parts, parts) for q in bh}

    def fold(parts_):
        out = []
        for bi in B_:
            acc = jnp.where(masks[0], parts_[(bi, 0)], 0.0)
            for h in range(1, RW_H):
                acc = jnp.where(masks[h], parts_[(bi, h)], acc)
            out.append(acc)
        return out

    a_til = fold(atp)
    u0 = fold(u0p)
    y0 = fold(y0p)
    st = [st_s[bi] for bi in B_]
    stb = [x.astype(bf16) for x in st]
    u = [_bdot(a_til[bi], stb[bi], NT) + u0[bi] for bi in B_]
    yst = [_bdot(r_bar[bi], stb[bi], NT) for bi in B_]
    ub = [x.astype(bf16) for x in u]
    yu = fold({q: _bdot(prb[q], ub[q[0]]) for q in bh})
    upd = [_bdot(jnp.concatenate([u[bi], v[bi]], axis=0), jnp.concatenate([b_hat[bi], k_hat[bi]], axis=0), TN)
           for bi in B_]
    same = jnp.right_shift(_iota((RW_W, RW_W), 0), 6) == jnp.right_shift(_iota((RW_W, RW_W), 1), 6)
    for bi in B_:
        st_s[bi] = st[bi] * jnp.exp(cum_c[bi]) + jnp.where(same, upd[bi], 0.0)

    inv_n = 1.0 / RW_N
    y = [yst[bi] + y0[bi] + yu[bi] for bi in B_]
    mean = [_dotx(x, bd1, 2, 1) * inv_n for x in y]
    bsum = [_dotx(r[bi] * k2[bi] * rk_ref[...], bd1, 2, 1) for bi in B_]
    d = [y[bi] - mean[bi] for bi in B_]
    var = [_dotx(x * x, bd1, 2, 1) * inv_n for x in d]
    for bi in B_:
        yn = d[bi] * lax.rsqrt(var[bi] + RW_GN_EPS) * lnw_ref[...] + lnb_ref[...]
        o_refs[bi][...] = ((yn + bsum[bi] * v[bi]) * gate[bi]).astype(o_refs[bi].dtype)

    @pl.when(t == nt - 1)
    def _():
        s_out_ref[...] = st_s[...]


def _rwkv(proj, row0, B, T, c, bb, parts, sh0, s0_bd, lw, out_dtype):
    nt = T // c
    rb0 = row0 // c
    pw = 256
    G = B // bb
    vec = lambda n: pl.BlockSpec((1, n), lambda g, t: (0, 0))
    mat = lambda r_, n: pl.BlockSpec((r_, n), lambda g, t: (0, 0))
    p_specs = [pl.BlockSpec((c, pw), lambda g, t, j=j, i=i: (rb0 + (g * bb + i) * nt + t, (CB_RW * LANES) // pw + j))
               for i in range(bb) for j in range(C_RWKV // pw)]
    hh = jnp.arange(RW_W) // RW_N
    bd1 = (hh[:, None] == hh[None, :]).astype(bf16)
    outs = pl.pallas_call(
        functools.partial(_rwkv_kernel, c=c, nt=nt, bb=bb, parts=parts),
        grid=(G, nt),
        in_specs=p_specs + [
            pl.BlockSpec((bb, 1, C_RWKV), lambda g, t: (g, 0, 0)),
            pl.BlockSpec((bb, RW_W, RW_W), lambda g, t: (g, 0, 0)),
            vec(C_RWKV), vec(RW_W), mat(LORA_W, RW_W), vec(RW_W), mat(LORA_A, RW_W), mat(LORA_G, RW_W),
            vec(RW_W), vec(RW_W), vec(RW_W), vec(RW_W), vec(RW_W), mat(RW_W, RW_W)],
        out_specs=[pl.BlockSpec((c, RW_W), lambda g, t: (g * nt + t, 0))] * bb + [
            pl.BlockSpec((bb, RW_W, RW_W), lambda g, t: (g, 0, 0)),
            pl.BlockSpec((bb, 1, C_RWKV), lambda g, t: (g, 0, 0))],
        out_shape=[jax.ShapeDtypeStruct((G * T, RW_W), out_dtype)] * bb + [
            jax.ShapeDtypeStruct((B, RW_W, RW_W), f32), jax.ShapeDtypeStruct((B, 1, C_RWKV), f32)],
        scratch_shapes=[pltpu.VMEM((bb, RW_W, RW_W), f32), pltpu.VMEM((bb, 1, C_RWKV), f32)],
        compiler_params=_cparams(("parallel", "arbitrary")), name="rwkv7",
    )(*([proj] * (bb * (C_RWKV // pw))), sh0.reshape(B, 1, C_RWKV), s0_bd,
      lw['mu'].reshape(1, -1), lw['w0'].reshape(1, -1), lw['w2'], lw['a0'].reshape(1, -1), lw['a2'], lw['g2'],
      lw['kk'].reshape(1, -1), lw['ka'].reshape(1, -1), lw['rk'].reshape(1, -1),
      lw['lnx_w'].reshape(1, -1), lw['lnx_b'].reshape(1, -1), bd1)
    return outs[:bb], outs[bb], outs[bb + 1]


def _to_blockdiag(s):
    B = s.shape[0]
    eye = jnp.eye(RW_H, dtype=s.dtype)
    return jnp.einsum('bhvk,hg->bhvgk', s, eye).reshape(B, RW_W, RW_W)


def _from_blockdiag(s):
    B = s.shape[0]
    s5 = s.reshape(B, RW_H, RW_N, RW_H, RW_N)
    idx = jnp.arange(RW_H)
    return s5[:, idx, :, idx, :].transpose(1, 0, 2, 3)


def _fox_prep_kernel(q_ref, k_ref, v_ref, qg_ref, kg_ref, qn_ref, kn_ref, kb_ref, vn_ref, vb_ref):
    def nrm(x, g):
        return x * lax.rsqrt(jnp.mean(x * x, axis=-1, keepdims=True) + EPS) * g
    qn_ref[...] = nrm(q_ref[...], qg_ref[...]).astype(qn_ref.dtype)
    kn = nrm(k_ref[...], kg_ref[...])
    kn_ref[...] = kn
    kb_ref[...] = kn.astype(kb_ref.dtype)
    v = v_ref[...]
    vn_ref[...] = v
    vb_ref[...] = v.astype(vb_ref.dtype)


def _fox_prep(proj, row0, rows, tt, qg, kg, lowp):
    rb0 = row0 // tt
    col = lambda cb: pl.BlockSpec((tt, FX_DH), lambda r, h, cb=cb: (rb0 + r, cb + h))
    out = pl.BlockSpec((tt, FX_DH), lambda r, h: (r, h))
    gsp = pl.BlockSpec((1, FX_DH), lambda r, h: (0, 0))
    sds = lambda dt: jax.ShapeDtypeStruct((rows, FX_W), dt)
    return pl.pallas_call(
        _fox_prep_kernel, grid=(rows // tt, FX_H),
        in_specs=[col(CB_FQ), col(CB_FK), col(CB_FV), gsp, gsp],
        out_specs=[out] * 5, out_shape=[sds(lowp), sds(f32), sds(lowp), sds(f32), sds(lowp)],
        compiler_params=_cparams(("parallel", "parallel")), name="fox_prep",
    )(proj, proj, proj, qg.reshape(1, FX_DH), kg.reshape(1, FX_DH))


def _fox_lf_kernel(ff_ref, b_ref, lf_ref, cum_ref, carry_s):
    tt = ff_ref.shape[0]

    @pl.when(pl.program_id(1) == 0)
    def _():
        carry_s[...] = jnp.zeros_like(carry_s)

    lf = jax.nn.log_sigmoid(ff_ref[...] + b_ref[...])
    tril = jnp.where(_iota((tt, tt), 0) >= _iota((tt, tt), 1), 1.0, 0.0).astype(bf16)
    cum = _dotx(tril, lf, 1, 3) + carry_s[...]
    carry_s[...] = cum[tt - 1:tt]
    lf_ref[...] = lf[:, :FX_H]
    cum_ref[...] = cum[:, :FX_H]


def _fox_lf(ffl, row0, B, T, tt, fbias):
    nt = T // tt
    rb0 = row0 // tt
    bias = jnp.zeros((1, LANES), f32).at[0, :FX_H].set(fbias)
    out = pl.BlockSpec((tt, FX_H), lambda b, t: (b * nt + t, 0))
    return pl.pallas_call(
        _fox_lf_kernel, grid=(B, nt),
        in_specs=[pl.BlockSpec((tt, LANES), lambda b, t: (rb0 + b * nt + t, 0)),
                  pl.BlockSpec((1, LANES), lambda b, t: (0, 0))],
        out_specs=[out, out], out_shape=[jax.ShapeDtypeStruct((B * T, FX_H), f32)] * 2,
        scratch_shapes=[pltpu.VMEM((1, LANES), f32)],
        compiler_params=_cparams(("parallel", "arbitrary")), name="fox_logf",
    )(ffl, bias)


def _fox_attn_kernel(q_ref, k_ref, v_ref, cq_ref, ck_ref, o_ref, m_s, l_s, acc_s, *, tq, tk, nk):
    qi = pl.program_id(2)
    ki = pl.program_id(3)

    @pl.when(ki == 0)
    def _():
        m_s[...] = jnp.full_like(m_s, -jnp.inf)
        l_s[...] = jnp.zeros_like(l_s)
        acc_s[...] = jnp.zeros_like(acc_s)

    nh = q_ref.shape[1] // FX_DH

    def block(masked):
        hs = [slice(i * FX_DH, (i + 1) * FX_DH) for i in range(nh)]
        ss = [lax.dot_general(q_ref[:, hs[i]], k_ref[:, hs[i]], NT, preferred_element_type=f32) for i in range(nh)]
        ps = []
        for i in range(nh):
            s = ss[i] * (FX_DH ** -0.5) + (cq_ref[i] - ck_ref[i])
            if masked:
                qpos = qi * tq + _iota((tq, tk), 0)
                kpos = ki * tk + _iota((tq, tk), 1)
                s = jnp.where(kpos <= qpos, s, -jnp.inf)
            m_old = m_s[i]
            m_new = jnp.maximum(m_old, jnp.max(s, axis=-1, keepdims=True))
            alpha = jnp.exp(m_old - m_new)
            p = jnp.exp(s - m_new)
            l_s[i] = alpha * l_s[i] + jnp.sum(p, axis=-1, keepdims=True)
            acc_s[:, hs[i]] = alpha * acc_s[:, hs[i]]
            m_s[i] = m_new
            ps.append(p.astype(bf16))
        for i in range(nh):
            acc_s[:, hs[i]] = acc_s[:, hs[i]] + jnp.dot(ps[i], v_ref[:, hs[i]], preferred_element_type=f32)

    first_q = qi * tq
    first_k = ki * tk
    fully_visible = first_k + (tk - 1) <= first_q
    pl.when(fully_visible)(functools.partial(block, False))
    pl.when(jnp.logical_and(jnp.logical_not(fully_visible), first_k <= first_q + (tq - 1)))(
        functools.partial(block, True))

    @pl.when(ki == nk - 1)
    def _():
        for i in range(nh):
            hs = slice(i * FX_DH, (i + 1) * FX_DH)
            o_ref[:, hs] = (acc_s[:, hs] / l_s[i]).astype(o_ref.dtype)


def _fox_attn(qn, kb, vb, cum, B, T, tq, tk, nh=2):
    nq, nk = T // tq, T // tk
    cq = cum.reshape(B, T, FX_H).transpose(0, 2, 1).reshape(B, FX_H, T, 1)
    ck = cq.reshape(B, FX_H, 1, T)
    last = lambda qi: (qi * tq + tq - 1) // tk
    w = nh * FX_DH
    kv = pl.BlockSpec((tk, w), lambda b, h, qi, ki: (b * nk + jnp.minimum(ki, last(qi)), h))
    return pl.pallas_call(
        functools.partial(_fox_attn_kernel, tq=tq, tk=tk, nk=nk),
        grid=(B, FX_H // nh, nq, nk),
        in_specs=[pl.BlockSpec((tq, w), lambda b, h, qi, ki: (b * nq + qi, h)), kv, kv,
                  pl.BlockSpec((None, nh, tq, 1), lambda b, h, qi, ki: (b, h, qi, 0)),
                  pl.BlockSpec((None, nh, 1, tk), lambda b, h, qi, ki: (b, h, 0, jnp.minimum(ki, last(qi))))],
        out_specs=pl.BlockSpec((tq, w), lambda b, h, qi, ki: (b * nq + qi, h)),
        out_shape=jax.ShapeDtypeStruct((B * T, FX_W), bf16),
        scratch_shapes=[pltpu.VMEM((nh, tq, 1), f32), pltpu.VMEM((nh, tq, 1), f32), pltpu.VMEM((tq, w), f32)],
        compiler_params=_cparams(("parallel", "parallel", "parallel", "arbitrary")), name="fox_attn",
    )(qn, kb, vb, cq, ck)


def _rep_rows(x, td):
    return jnp.concatenate([jnp.broadcast_to(x[h:h + 1], (td, x.shape[1])) for h in range(x.shape[0])], axis=0)


def _fox_dec_kernel(pt_ref, q_ref, kn_ref, vn_ref, cqc_ref, cqr_ref, *refs, td, n_steps, pps):
    kc_refs = refs[:pps]
    vc_refs = refs[pps:2 * pps]
    lft_refs = refs[2 * pps:3 * pps]
    o_ref, m_s, l_s, acc_s, suf_s, qe_s = refs[3 * pps:]
    j = pl.program_id(1)
    nr = FX_H * td
    scale = FX_DH ** -0.5
    ltd = int(math.log2(td))
    hm = jnp.right_shift(_iota((nr, FX_W), 0), ltd) == jnp.right_shift(_iota((nr, FX_W), 1), 7)

    def update(s, vbfs):
        m_old = m_s[...]
        m_new = jnp.maximum(m_old, jnp.max(s, axis=-1, keepdims=True))
        alpha = jnp.exp(m_old - m_new)
        p = jnp.exp(s - m_new)
        l_s[...] = alpha * l_s[...] + jnp.sum(p, axis=-1, keepdims=True)
        pv = None
        for i, vbf in enumerate(vbfs):
            t = jnp.dot(p[:, i * PAGE:(i + 1) * PAGE].astype(bf16), vbf, preferred_element_type=f32)
            pv = t if pv is None else pv + t
        acc_s[...] = alpha * acc_s[...] + pv
        m_s[...] = m_new

    @pl.when(j == 0)
    def _():
        q = q_ref[...]
        qe = jnp.where(hm, jnp.concatenate([q] * FX_H, axis=0), 0.0).astype(bf16)
        qe_s[...] = qe
        pad = jnp.zeros((PAGE - td, FX_W), f32)
        kp = jnp.concatenate([kn_ref[...], pad], axis=0).astype(bf16)
        vp = jnp.concatenate([vn_ref[...], pad], axis=0).astype(bf16)
        s = lax.dot_general(qe, kp, NT, preferred_element_type=f32) * scale
        s = s + (cqc_ref[...] - _rep_rows(cqr_ref[...], td))
        kpos = _iota((nr, PAGE), 1)
        qpos = jnp.bitwise_and(_iota((nr, PAGE), 0), td - 1)
        s = jnp.where(kpos <= qpos, s, -jnp.inf)
        m_s[...] = jnp.full_like(m_s, -jnp.inf)
        l_s[...] = jnp.zeros_like(l_s)
        acc_s[...] = jnp.zeros_like(acc_s)
        suf_s[...] = jnp.zeros_like(suf_s)
        update(s, [vp])

    @pl.when(j > 0)
    def _():
        later = jnp.where(_iota((PAGE, PAGE), 0) > _iota((PAGE, PAGE), 1), 1.0, 0.0).astype(bf16)
        qe = qe_s[...]
        cq = cqc_ref[...]
        after = suf_s[...]
        parts = []
        for i in range(pps):
            lft = lft_refs[i][...]
            suf = _dotx(lft, later, 3, 1) + after
            after = after + jnp.sum(lft, axis=-1, keepdims=True)
            s = lax.dot_general(qe, kc_refs[i][...].astype(bf16), NT, preferred_element_type=f32) * scale
            parts.append(s + (cq + _rep_rows(suf, td)))
        suf_s[...] = after
        update(jnp.concatenate(parts, axis=1), [r[...].astype(bf16) for r in vc_refs])

    @pl.when(j == n_steps - 1)
    def _():
        a = jnp.where(hm, acc_s[...] / l_s[...], 0.0)
        o = a[0:td]
        for h in range(1, FX_H):
            o = o + a[h * td:(h + 1) * td]
        o_ref[...] = o.astype(o_ref.dtype)


def _fox_decode(qn, kn, vn, cnew, kcache, vcache, lft, page_table, page0, td):
    Bd, n_pages = page_table.shape
    pps = 4 if n_pages % 4 == 0 else (2 if n_pages % 2 == 0 else 1)
    n_steps = n_pages // pps + 1
    nr = FX_H * td
    c3 = cnew.reshape(Bd, td, FX_H).transpose(0, 2, 1)
    cqc = c3.reshape(Bd, nr, 1)
    cqr = jnp.zeros((Bd, FX_H, PAGE), f32).at[:, :, :td].set(c3)

    def page(i):
        return lambda b, j, pt: (page0 + pt[b, n_pages - 1 - i - (jnp.maximum(j, 1) - 1) * pps], 0, 0)

    tok = pl.BlockSpec((td, FX_W), lambda b, j, pt: (b, 0))
    kvs = [pl.BlockSpec((None, PAGE, FX_W), page(i)) for i in range(pps)]
    lfs = [pl.BlockSpec((None, FX_H, PAGE), page(i)) for i in range(pps)]
    grid_spec = pltpu.PrefetchScalarGridSpec(
        num_scalar_prefetch=1, grid=(Bd, n_steps),
        in_specs=[tok, tok, tok,
                  pl.BlockSpec((None, nr, 1), lambda b, j, pt: (b, 0, 0)),
                  pl.BlockSpec((None, FX_H, PAGE), lambda b, j, pt: (b, 0, 0))] + kvs + kvs + lfs,
        out_specs=pl.BlockSpec((td, FX_W), lambda b, j, pt: (b, 0)),
        scratch_shapes=[pltpu.VMEM((nr, 1), f32), pltpu.VMEM((nr, 1), f32), pltpu.VMEM((nr, FX_W), f32),
                        pltpu.VMEM((FX_H, PAGE), f32), pltpu.VMEM((nr, FX_W), bf16)])
    return pl.pallas_call(
        functools.partial(_fox_dec_kernel, td=td, n_steps=n_steps, pps=pps), grid_spec=grid_spec,
        out_shape=jax.ShapeDtypeStruct((Bd * td, FX_W), f32),
        compiler_params=_cparams(("parallel", "arbitrary")), name="fox_decode",
    )(page_table, qn, kn, vn, cqc, cqr, *([kcache] * pps), *([vcache] * pps), *([lft] * pps))


def _moe_plan_kernel(r_ref, pos_ref, te_ref, cs_s, *, rb, tm_e):
    M = r_ref.shape[0]
    nb = M // rb
    lane = _iota((rb, LANES), 1)
    strict = jnp.where(_iota((rb, rb), 0) > _iota((rb, rb), 1), 1.0, 0.0).astype(bf16)

    def onehots(blk):
        r = r_ref[blk * rb:(blk + 1) * rb, :]
        i1 = r[:, 0:1].astype(i32)
        i2 = r[:, 1:2].astype(i32)
        return lane == i1, lane == i2

    carry = jnp.zeros((1, LANES), f32)
    for blk in range(nb):
        o1, o2 = onehots(blk)
        oh = jnp.where(o1, 1.0, jnp.where(o2, 1.0, 0.0))
        cs_s[blk * rb:(blk + 1) * rb, :] = _dotx(strict, oh.astype(bf16), 1, 1) + carry
        carry = carry + jnp.sum(oh, axis=0, keepdims=True)
    padded = jnp.floor((carry + (tm_e - 1)) * (1.0 / tm_e)) * tm_e
    before = jnp.where(_iota((LANES, LANES), 0) < _iota((LANES, LANES), 1), 1.0, 0.0).astype(bf16)
    off = _dotx(jnp.broadcast_to(padded, (8, LANES)), before, 3, 1)[0:1]
    end = off + padded
    for blk in range(nb):
        o1, o2 = onehots(blk)
        base = off + cs_s[blk * rb:(blk + 1) * rb, :]
        p1 = jnp.sum(jnp.where(o1, base, 0.0), axis=-1, keepdims=True)
        p2 = jnp.sum(jnp.where(o2, base, 0.0), axis=-1, keepdims=True)
        pos_ref[blk * rb:(blk + 1) * rb, :] = jnp.where(lane == 0, p1, jnp.where(lane == 1, p2, 0.0)).astype(i32)
    l1 = _iota((1, LANES), 1)
    start = (l1 * tm_e).astype(f32)
    te = jnp.zeros((1, LANES), f32)
    for e in range(N_EXPERTS):
        end_e = jnp.sum(jnp.where(l1 == e, end, 0.0), axis=-1, keepdims=True)
        te = te + jnp.where(end_e <= start, 1.0, 0.0)
    total = jnp.sum(jnp.where(l1 == N_EXPERTS - 1, end, 0.0), axis=-1, keepdims=True)
    te = jnp.minimum(te, N_EXPERTS - 1.0)
    te_ref[...] = jnp.where(l1 == LANES - 1, total * (1.0 / tm_e), te).astype(i32)


def _moe_plan(route, tm_e, rb):
    M = route.shape[0]
    return pl.pallas_call(
        functools.partial(_moe_plan_kernel, rb=rb, tm_e=tm_e),
        out_shape=[jax.ShapeDtypeStruct((M, LANES), i32), jax.ShapeDtypeStruct((1, LANES), i32)],
        scratch_shapes=[pltpu.VMEM((M, LANES), f32)],
        compiler_params=pltpu.CompilerParams(vmem_limit_bytes=VMEM_LIMIT_BYTES), name="moe_plan")(route)


def _moe_gather_kernel(pos_ref, te_ref, u_ref, xs_ref, src_s, buf, sem, *, tm_e, n_pairs):
    t = pl.program_id(0)
    rows = src_s.shape[0]

    @pl.when(t == 0)
    def _():
        def clear(i, c):
            src_s[i] = 0
            return c

        def put(i, c):
            src_s[pos_ref[i]] = jnp.right_shift(i, 1)
            return c

        lax.fori_loop(0, rows, clear, 0, unroll=8)
        lax.fori_loop(0, n_pairs, put, 0, unroll=8)

    def copy(r):
        return pltpu.make_async_copy(u_ref.at[src_s[t * tm_e + r]], buf.at[r], sem)

    @pl.when(t < te_ref[LANES - 1])
    def _():
        def start(r, c):
            copy(r).start()
            return c

        def wait(r, c):
            copy(r).wait()
            return c

        lax.fori_loop(0, tm_e, start, 0)
        lax.fori_loop(0, tm_e, wait, 0)
        for s in range(buf.shape[1]):
            xs_ref[:, s * LANES:(s + 1) * LANES] = buf[:, s, :].astype(xs_ref.dtype)

    @pl.when(t >= te_ref[LANES - 1])
    def _():
        xs_ref[...] = jnp.zeros_like(xs_ref)


def _moe_gather(u3, pos_flat, te, rows, tm_e):
    M, S, _ = u3.shape
    grid_spec = pltpu.PrefetchScalarGridSpec(
        num_scalar_prefetch=2, grid=(rows // tm_e,),
        in_specs=[pl.BlockSpec(memory_space=pl.ANY)],
        out_specs=pl.BlockSpec((tm_e, S * LANES), lambda t, p, te_: (t, 0)),
        scratch_shapes=[pltpu.SMEM((rows,), i32), pltpu.VMEM((tm_e, S, LANES), f32), pltpu.SemaphoreType.DMA(())])
    return pl.pallas_call(
        functools.partial(_moe_gather_kernel, tm_e=tm_e, n_pairs=2 * M), grid_spec=grid_spec,
        out_shape=jax.ShapeDtypeStruct((rows, S * LANES), bf16),
        compiler_params=_cparams(("arbitrary",)), name="moe_gather")(pos_flat, te, u3)


def _moe_mm_kernel(te_ref, *refs, glu):
    it = iter(refs)
    a_ref = next(it)
    w_ref = next(it)
    w3_ref = next(it) if glu else None
    o_ref = next(it)
    wb_ref = next(it)
    wb3_ref = next(it) if glu else None
    t = pl.program_id(1)
    e = te_ref[t]
    e_prev = te_ref[jnp.maximum(t - 1, 0)]

    @pl.when((t == 0) | (e != e_prev))
    def _():
        wb_ref[...] = w_ref[...].astype(bf16)
        if glu:
            wb3_ref[...] = w3_ref[...].astype(bf16)

    @pl.when(t < te_ref[LANES - 1])
    def _():
        a = a_ref[...]
        y = jnp.dot(a, wb_ref[...], preferred_element_type=f32)
        if glu:
            y = jax.nn.silu(y) * jnp.dot(a, wb3_ref[...], preferred_element_type=f32)
        if len(o_ref.shape) == 3:
            for s in range(o_ref.shape[1]):
                o_ref[:, s, :] = y[:, s * LANES:(s + 1) * LANES].astype(o_ref.dtype)
        else:
            o_ref[...] = y.astype(o_ref.dtype)

    @pl.when(t >= te_ref[LANES - 1])
    def _():
        o_ref[...] = jnp.zeros_like(o_ref)


def _moe_matmul(a, te, w, fi, tm_e, tn, w3=None, out_dtype=f32, slabs=False, name="moe_matmul"):
    R, K = a.shape
    N = w.shape[-1]
    glu = w3 is not None
    w_spec = pl.BlockSpec((None, None, K, tn), lambda n, t, te_: (fi, te_[t], 0, n))
    in_specs = [pl.BlockSpec((tm_e, K), lambda n, t, te_: (t, 0)), w_spec] + ([w_spec] if glu else [])
    args = [a, w] + ([w3] if glu else [])
    scratch = [pltpu.VMEM((K, tn), bf16)] * (2 if glu else 1)
    if slabs:
        out_shape = jax.ShapeDtypeStruct((R, N // tn, tn // LANES, LANES), out_dtype)
        out_spec = pl.BlockSpec((tm_e, None, tn // LANES, LANES), lambda n, t, te_: (t, n, 0, 0))
    else:
        out_shape = jax.ShapeDtypeStruct((R, N), out_dtype)
        out_spec = pl.BlockSpec((tm_e, tn), lambda n, t, te_: (t, n))
    grid_spec = pltpu.PrefetchScalarGridSpec(
        num_scalar_prefetch=1, grid=(N // tn, R // tm_e), in_specs=in_specs,
        out_specs=out_spec, scratch_shapes=scratch)
    return pl.pallas_call(
        functools.partial(_moe_mm_kernel, glu=glu), grid_spec=grid_spec, out_shape=out_shape,
        compiler_params=_cparams(("arbitrary", "arbitrary")), name=name)(te, *args)


def _moe_combine_kernel(pos_ref, h_ref, r_ref, y_ref, o_ref, buf0, buf1, sem, *, tb):
    i0 = pl.program_id(0) * tb
    bufs = (buf0, buf1)

    def copy(i, j):
        return pltpu.make_async_copy(y_ref.at[pos_ref[2 * (i0 + i) + j]], bufs[j].at[i], sem)

    def start(i, c):
        copy(i, 0).start()
        copy(i, 1).start()
        return c

    def wait(i, c):
        copy(i, 0).wait()
        copy(i, 1).wait()
        return c

    lax.fori_loop(0, tb, start, 0)
    lax.fori_loop(0, tb, wait, 0)
    r = r_ref[...]
    g1 = r[:, 2:3]
    g2 = r[:, 3:4]
    nn, ns = buf0.shape[1], buf0.shape[2]
    for n in range(nn):
        for s in range(ns):
            c0 = (n * ns + s) * LANES
            o_ref[:, c0:c0 + LANES] = h_ref[:, c0:c0 + LANES] + (g1 * buf0[:, n, s, :] + g2 * buf1[:, n, s, :])


def _moe_combine(h, route, y3, pos_flat, tb):
    M, D = h.shape
    grid_spec = pltpu.PrefetchScalarGridSpec(
        num_scalar_prefetch=1, grid=(M // tb,),
        in_specs=[pl.BlockSpec((tb, D), lambda m, p: (m, 0)), pl.BlockSpec((tb, LANES), lambda m, p: (m, 0)),
                  pl.BlockSpec(memory_space=pl.ANY)],
        out_specs=pl.BlockSpec((tb, D), lambda m, p: (m, 0)),
        scratch_shapes=[pltpu.VMEM((tb,) + y3.shape[1:], f32), pltpu.VMEM((tb,) + y3.shape[1:], f32),
                        pltpu.SemaphoreType.DMA(())])
    return pl.pallas_call(
        functools.partial(_moe_combine_kernel, tb=tb), grid_spec=grid_spec,
        out_shape=jax.ShapeDtypeStruct((M, D), f32),
        compiler_params=_cparams(("arbitrary",)), name="moe_combine")(pos_flat, h, route, y3)


def _moe(h, u3, route, w1, w3, w2, fi, tm, tm_e=256):
    M = h.shape[0]
    rows = (pl.cdiv(2 * M, tm_e) + N_EXPERTS) * tm_e
    pos, te = _moe_plan(route, tm_e, tm)
    pos_flat = pos[:, :2].reshape(-1)
    te = te.reshape(-1)
    xs = _moe_gather(u3, pos_flat, te, rows, tm_e)
    act = _moe_matmul(xs, te, w1, fi, tm_e, 1024, w3=w3, out_dtype=bf16, name="moe_up")
    y = _moe_matmul(act, te, w2, fi, tm_e, 512, slabs=True, name="moe_down")
    return _moe_combine(h, route, y, pos_flat, tm // 2)


def kernel(x_prompt, x_sample, cache_fox_k, cache_fox_v, cache_fox_logf, page_table, state_hgrn, state_rwkv, state_rwkv_shift, norm_mix, w_in, w_out, hgrn_lb, hgrn_norm, rwkv_mu, rwkv_w0, rwkv_w2, rwkv_a0, rwkv_a2, rwkv_g2, rwkv_kk, rwkv_ka, rwkv_rk, rwkv_lnx_w, rwkv_lnx_b, fox_qnorm, fox_knorm, fox_fbias, norm_ffn, ffn_w1, ffn_w3, ffn_w2, moe_router, moe_w1, moe_w3, moe_w2):
    B, T, D = x_prompt.shape
    Bd, Td, _ = x_sample.shape
    depth = w_in.shape[0]
    n_pool = cache_fox_k.shape[1]
    Mp, Ms = B * T, Bd * Td
    M = Mp + Ms
    tm = 688 if M % 688 == 0 else M
    d_ff = ffn_w1.shape[-1]

    h = jnp.concatenate([x_prompt.reshape(Mp, D), x_sample.reshape(Ms, D)], axis=0)
    lb_all = jnp.cumsum(jax.nn.softmax(hgrn_lb.astype(f32), axis=0), axis=0)
    lb_all = lb_all - lb_all[:1]
    kcache = cache_fox_k.reshape(depth * n_pool, PAGE, FX_W)
    vcache = cache_fox_v.reshape(depth * n_pool, PAGE, FX_W)
    lft = cache_fox_logf.transpose(0, 1, 3, 2).reshape(depth * n_pool, FX_H, PAGE)
    zeros_hg = jnp.zeros((B, HG_H, HG_D, HG_D), f32)
    zeros_rw = jnp.zeros((B, RW_W, RW_W), f32)
    zeros_sh = jnp.zeros((B, C_RWKV), f32)
    c_p = 64 if T % 64 == 0 else T
    tb_p = 256 if T % 256 == 0 else T
    tt_p = 512 if T % 512 == 0 else T

    outs = [[] for _ in range(12)]
    for l in range(depth):
        fi = l // 2
        rw = dict(mu=rwkv_mu[l], w0=rwkv_w0[l], w2=rwkv_w2[l], a0=rwkv_a0[l], a2=rwkv_a2[l], g2=rwkv_g2[l],
                  kk=rwkv_kk[l], ka=rwkv_ka[l], rk=rwkv_rk[l], lnx_w=rwkv_lnx_w[l], lnx_b=rwkv_lnx_b[l])
        u, ffl = _rmsnorm(h, norm_mix[l], w_small=w_in[l][:, N_PROJ:], tm=tm)
        proj = _matmul(u, w_in, l, N_PROJ, 768, tm, name="mm_in")

        o_hg, s_hg = _hgrn(proj, 0, B, T, lb_all[l], hgrn_norm[l], zeros_hg, 16 if tb_p % 16 == 0 else tb_p, tb_p, bf16)
        o_rw, s_rw, sh = _rwkv(proj, 0, B, T, c_p, B, 1, zeros_sh, zeros_rw, rw, bf16)
        o_rw = jnp.concatenate(o_rw, axis=0)
        qn, kn, kb, vn, vb = _fox_prep(proj, 0, Mp, min(1024, Mp), fox_qnorm[l], fox_knorm[l], bf16)
        lf, cum = _fox_lf(ffl, 0, B, T, tt_p, fox_fbias[l])
        o_fx = _fox_attn(qn, kb, vb, cum, B, T, tt_p, tt_p)
        for i, x in enumerate((kn.reshape(B, T, FX_H, FX_DH), vn.reshape(B, T, FX_H, FX_DH),
                               lf.reshape(B, T, FX_H), s_hg, _from_blockdiag(s_rw), sh.reshape(B, C_RWKV))):
            outs[i].append(x)

        o_hg_s, s_hg_s = _hgrn(proj, Mp, Bd, Td, lb_all[l], hgrn_norm[l], state_hgrn[l], Td, Td, f32)
        (o_rw_s,), s_rw_s, sh_s = _rwkv(proj, Mp, Bd, Td, Td, 1, 2, state_rwkv_shift[l], _to_blockdiag(state_rwkv[l]),
                                        rw, f32)
        qn_s, kn_s, _, vn_s, _ = _fox_prep(proj, Mp, Ms, Ms, fox_qnorm[l], fox_knorm[l], f32)
        lf_s, cnew = _fox_lf(ffl, Mp, Bd, Td, Td, fox_fbias[l])
        o_fx_s = _fox_decode(qn_s, kn_s, vn_s, cnew, kcache, vcache, lft, page_table, l * n_pool, Td)
        for i, x in enumerate((kn_s.reshape(Bd, Td, FX_H, FX_DH), vn_s.reshape(Bd, Td, FX_H, FX_DH),
                               lf_s.reshape(Bd, Td, FX_H), s_hg_s, _from_blockdiag(s_rw_s), sh_s.reshape(Bd, C_RWKV))):
            outs[6 + i].append(x)

        mix = jnp.concatenate([jnp.concatenate([o_hg, o_rw, o_fx], axis=1),
                               jnp.concatenate([o_hg_s, o_rw_s, o_fx_s], axis=1).astype(bf16)], axis=0)
        h = _matmul(mix, w_out, l, D, 512, tm, res=h, name="mm_out")
        if l % 2 == 0:
            u2 = _rmsnorm(h, norm_ffn[l], tm=tm)
            act = _matmul(u2, ffn_w1, fi, d_ff, 512, tm, w3=ffn_w3, out_dtype=bf16, name="ffn_up")
            h = _matmul(act, ffn_w2, fi, D, 512, tm, res=h, name="ffn_down")
        else:
            u3, route = _rmsnorm(h, norm_ffn[l], w_small=moe_router[fi], route=True, tm=tm, slabs=True)
            h = _moe(h, u3, route, moe_w1, moe_w3, moe_w2, fi, tm)

    st = [jnp.stack(o) for o in outs]
    return (h[:Mp].reshape(B, T, D), h[Mp:].reshape(Bd, Td, D), *st)
```

```python
import functools
import math

import jax
import jax.numpy as jnp
from jax import lax
from jax.experimental import pallas as pl
from jax.experimental.pallas import tpu as pltpu

f32 = jnp.float32
bf16 = jnp.bfloat16
i32 = jnp.int32

D_MODEL = 2048
DEPTH = 4
PAGE = 128
HG_H, HG_D = 4, 128
HG_W = HG_H * HG_D
RW_H, RW_N = 8, 64
RW_W = RW_H * RW_N
LORA_W, LORA_A, LORA_G = 64, 64, 128
C_RWKV = 3 * RW_W + LORA_W + LORA_A + LORA_G
FX_H, FX_DH = 8, 128
FX_W = FX_H * FX_DH
N_EXPERTS = 8
EPS = 1e-6
RW_GN_EPS = 64e-5
CB_HG = 0
CB_RW = (4 * HG_W) // 128
CB_FQ = CB_RW + C_RWKV // 128
CB_FK = CB_FQ + FX_H
CB_FV = CB_FK + FX_H
N_PROJ = (CB_FV + FX_H) * 128

LANES = 128
VMEM_LIMIT_BYTES = 56 * 1024 * 1024

NT = (((1,), (1,)), ((), ()))
TN = (((0,), (0,)), ((), ()))


def _cparams(sem):
    return pltpu.CompilerParams(dimension_semantics=sem, vmem_limit_bytes=VMEM_LIMIT_BYTES)


def _split(x, n):
    parts, r = [], x
    for i in range(n):
        p = r.astype(bf16)
        parts.append(p)
        if i + 1 < n:
            r = r - p.astype(f32)
    return parts


def _dotx(a, b, na, nb, dims=None):
    ap = _split(a, na) if a.dtype != bf16 else [a]
    bp = _split(b, nb) if b.dtype != bf16 else [b]
    n = max(len(ap), len(bp))
    acc = None
    for i, x in enumerate(ap):
        for j, y in enumerate(bp):
            if i + j < n:
                t = (jnp.dot(x, y, preferred_element_type=f32) if dims is None
                     else lax.dot_general(x, y, dims, preferred_element_type=f32))
                acc = t if acc is None else acc + t
    return acc


def _bdot(a, b, dims=None):
    return _dotx(a.astype(bf16), b.astype(bf16), 1, 1, dims)


def _iota(shape, axis):
    return lax.broadcasted_iota(i32, shape, axis)


def _route(lg):
    lane = _iota(lg.shape, 1)
    l1 = jnp.where(lane < N_EXPERTS, lg, -jnp.inf)
    m1 = jnp.max(l1, axis=-1, keepdims=True)
    i1 = jnp.min(jnp.where(l1 == m1, lane, LANES), axis=-1, keepdims=True)
    l2 = jnp.where(lane == i1, -jnp.inf, l1)
    m2 = jnp.max(l2, axis=-1, keepdims=True)
    i2 = jnp.min(jnp.where(l2 == m2, lane, LANES), axis=-1, keepdims=True)
    e = jnp.exp(m2 - m1)
    den = 1.0 + e
    g1 = 1.0 / den
    g2 = e / den
    return jnp.where(lane == 0, i1.astype(f32),
                     jnp.where(lane == 1, i2.astype(f32),
                               jnp.where(lane == 2, g1, jnp.where(lane == 3, g2, 0.0))))


def _norm_kernel(*refs, has_small, route):
    if has_small:
        x_ref, g_ref, ws_ref, u_ref, s_ref = refs
    else:
        x_ref, g_ref, u_ref = refs
    x = x_ref[...]
    ms = jnp.mean(x * x, axis=-1, keepdims=True)
    u = x * lax.rsqrt(ms + EPS) * g_ref[...]
    if u_ref.dtype == f32:
        ns = u.shape[1] // LANES
        for s in range(ns):
            u_ref[pl.ds(s, u.shape[0], stride=ns), :] = u[:, s * LANES:(s + 1) * LANES]
    else:
        u_ref[...] = u.astype(bf16)
    if has_small:
        s = _dotx(u, ws_ref[...], 2, 2)
        s_ref[...] = _route(s) if route else s


def _rmsnorm(x, g, w_small=None, route=False, tm=688, slabs=False):
    M, D = x.shape
    has_small = w_small is not None
    in_specs = [pl.BlockSpec((tm, D), lambda m: (m, 0)), pl.BlockSpec((1, D), lambda m: (0, 0))]
    args = [x, g.reshape(1, D)]
    if slabs:
        out_shape = [jax.ShapeDtypeStruct((M * (D // LANES), LANES), f32)]
        out_specs = [pl.BlockSpec((tm * (D // LANES), LANES), lambda m: (m, 0))]
    else:
        out_shape = [jax.ShapeDtypeStruct((M, D), bf16)]
        out_specs = [pl.BlockSpec((tm, D), lambda m: (m, 0))]
    if has_small:
        ws = jnp.zeros((D, LANES), f32).at[:, :w_small.shape[1]].set(w_small)
        in_specs.append(pl.BlockSpec((D, LANES), lambda m: (0, 0)))
        args.append(ws)
        out_shape.append(jax.ShapeDtypeStruct((M, LANES), f32))
        out_specs.append(pl.BlockSpec((tm, LANES), lambda m: (m, 0)))
    outs = pl.pallas_call(
        functools.partial(_norm_kernel, has_small=has_small, route=route),
        grid=(M // tm,), in_specs=in_specs, out_specs=out_specs, out_shape=out_shape,
        compiler_params=_cparams(("parallel",)), name="rmsnorm")(*args)
    outs = list(outs)
    if slabs:
        outs[0] = outs[0].reshape(M, D // LANES, LANES)
    return outs if has_small else outs[0]


def _mm_kernel(*refs, glu, has_res):
    it = iter(refs)
    a_ref = next(it)
    w_ref = next(it)
    w3_ref = next(it) if glu else None
    r_ref = next(it) if has_res else None
    o_ref = next(it)
    wb_ref = next(it)
    wb3_ref = next(it) if glu else None

    @pl.when(pl.program_id(1) == 0)
    def _():
        wb_ref[...] = w_ref[...].astype(bf16)
        if glu:
            wb3_ref[...] = w3_ref[...].astype(bf16)

    a = a_ref[...]
    y = jnp.dot(a, wb_ref[...], preferred_element_type=f32)
    if glu:
        y = jax.nn.silu(y) * jnp.dot(a, wb3_ref[...], preferred_element_type=f32)
    if has_res:
        y = r_ref[...] + y
    o_ref[...] = y.astype(o_ref.dtype)


def _matmul(a, w, lidx, n_out, tn, tm, w3=None, res=None, out_dtype=f32, name="matmul"):
    M, K = a.shape
    glu = w3 is not None
    has_res = res is not None
    w_spec = pl.BlockSpec((None, K, tn), lambda n, m: (lidx, 0, n))
    in_specs = [pl.BlockSpec((tm, K), lambda n, m: (m, 0)), w_spec]
    args = [a, w]
    scratch = [pltpu.VMEM((K, tn), bf16)]
    if glu:
        in_specs.append(w_spec)
        args.append(w3)
        scratch.append(pltpu.VMEM((K, tn), bf16))
    if has_res:
        in_specs.append(pl.BlockSpec((tm, tn), lambda n, m: (m, n)))
        args.append(res)
    return pl.pallas_call(
        functools.partial(_mm_kernel, glu=glu, has_res=has_res),
        grid=(pl.cdiv(n_out, tn), M // tm), in_specs=in_specs,
        out_specs=pl.BlockSpec((tm, tn), lambda n, m: (m, n)),
        out_shape=jax.ShapeDtypeStruct((M, n_out), out_dtype), scratch_shapes=scratch,
        compiler_params=_cparams(("arbitrary", "arbitrary")), name=name)(*args)


def _mm_out_kernel(*refs, n_a):
    a_refs = refs[:n_a]
    w_ref, h_ref, o_ref, wb_ref = refs[n_a:]

    @pl.when(pl.program_id(1) == 0)
    def _():
        wb_ref[...] = w_ref[...].astype(bf16)

    acc = h_ref[...]
    k0 = 0
    for a_ref in a_refs:
        kw = a_ref.shape[1]
        acc = acc + jnp.dot(a_ref[...].astype(bf16), wb_ref[k0:k0 + kw, :], preferred_element_type=f32)
        k0 += kw
    o_ref[...] = acc


def _mm_out(h, parts, row0, tm, w, lidx, tn=512):
    rows = parts[0].shape[0]
    K, N = w.shape[1], w.shape[2]
    rb0 = row0 // tm
    tile = pl.BlockSpec((tm, tn), lambda n, m: (rb0 + m, n))
    in_specs = [pl.BlockSpec((tm, a.shape[1]), lambda n, m: (m, 0)) for a in parts]
    in_specs += [pl.BlockSpec((None, K, tn), lambda n, m: (lidx, 0, n)), tile]
    return pl.pallas_call(
        functools.partial(_mm_out_kernel, n_a=len(parts)),
        grid=(N // tn, rows // tm), in_specs=in_specs, out_specs=tile,
        out_shape=jax.ShapeDtypeStruct(h.shape, h.dtype), scratch_shapes=[pltpu.VMEM((K, tn), bf16)],
        input_output_aliases={len(parts) + 1: 0},
        compiler_params=_cparams(("arbitrary", "arbitrary")), name="mm_out")(*parts, w, h)


def _hgrn_kernel(q_ref, f_ref, i_ref, g_ref, lb_ref, nw_ref, s0_ref, o_ref, s_out_ref,
                 st_s, qh_s, kh_s, gr_s, o_s, *, c, nt):
    t = pl.program_id(2)
    tb = q_ref.shape[0]
    lc = int(math.log2(c))

    @pl.when(t == 0)
    def _():
        st_s[...] = s0_ref[...].T

    lb = lb_ref[...]
    fl = f_ref[...]
    lf = jnp.logaddexp(jnp.log(lb), jnp.log1p(-lb) + jax.nn.log_sigmoid(fl))
    kh_s[...] = (1.0 - lb) * jax.nn.sigmoid(-fl)
    qh_s[...] = jax.nn.silu(q_ref[...])
    row = _iota((tb, tb), 0)
    col = _iota((tb, tb), 1)
    same = jnp.right_shift(row, lc) == jnp.right_shift(col, lc)
    bd = jnp.where(same, jnp.where(col <= row, 1.0, 0.0), 0.0).astype(bf16)
    gr_s[...] = _dotx(bd, lf, 1, 3)
    ones = jnp.ones((HG_D, HG_D), bf16)
    rows = _iota((c, HG_D), 0)

    nsb = tb // c
    incs, decs = [], []
    for j in range(nsb):
        r0 = j * c
        qs = qh_s[pl.ds(r0, c), :]
        ks = kh_s[pl.ds(r0, c), :]
        gs = gr_s[pl.ds(r0, c), :]
        vs = i_ref[pl.ds(r0, c), :]
        glast = gr_s[pl.ds(r0 + (c - 1), 1), :]
        xs = []
        for s in range(c):
            m = rows >= s
            e = jnp.exp(jnp.where(m, gs - gs[s:s + 1, :], 0.0))
            xs.append(jnp.where(m, qs * ks[s:s + 1, :] * e, 0.0))
        p = _dotx(jnp.concatenate(xs, axis=0), ones, 2, 1)
        o = p[0:c, :] * vs[0:1, :]
        for s in range(1, c):
            o = o + p[s * c:(s + 1) * c, :] * vs[s:s + 1, :]
        o_s[pl.ds(r0, c), :] = o
        incs.append(_bdot(vs, ks * jnp.exp(glast - gs), TN))
        decs.append(jnp.exp(glast))
    st = st_s[...]
    for j in range(nsb):
        r0 = j * c
        qt = qh_s[pl.ds(r0, c), :] * jnp.exp(gr_s[pl.ds(r0, c), :])
        o_s[pl.ds(r0, c), :] = o_s[pl.ds(r0, c), :] + _bdot(qt, st, NT)
        st = st * decs[j] + incs[j]

    st_s[...] = st
    o = o_s[...]
    y = o * lax.rsqrt(jnp.mean(o * o, axis=-1, keepdims=True) + EPS) * nw_ref[...]
    o_ref[...] = (y * jax.nn.silu(g_ref[...])).astype(o_ref.dtype)

    @pl.when(t == nt - 1)
    def _():
        s_out_ref[...] = st_s[...].T


def _hgrn(proj, row0, B, T, lb, nw, s0, c, tb, out_dtype):
    nt = T // tb
    rb0 = row0 // tb

    def col(k):
        return pl.BlockSpec((tb, HG_D), lambda b, h, t, k=k: (rb0 + b * nt + t, CB_HG + k * HG_H + h))

    st_spec = pl.BlockSpec((None, None, HG_D, HG_D), lambda b, h, t: (b, h, 0, 0))
    return pl.pallas_call(
        functools.partial(_hgrn_kernel, c=c, nt=nt),
        grid=(B, HG_H, nt),
        in_specs=[col(0), col(1), col(2), col(3),
                  pl.BlockSpec((None, 1, HG_D), lambda b, h, t: (h, 0, 0)),
                  pl.BlockSpec((1, HG_D), lambda b, h, t: (0, 0)),
                  st_spec],
        out_specs=[pl.BlockSpec((tb, HG_D), lambda b, h, t: (b * nt + t, h)), st_spec],
        out_shape=[jax.ShapeDtypeStruct((B * T, HG_W), out_dtype),
                   jax.ShapeDtypeStruct((B, HG_H, HG_D, HG_D), f32)],
        scratch_shapes=[pltpu.VMEM((HG_D, HG_D), f32)] + [pltpu.VMEM((tb, HG_D), f32)] * 4,
        compiler_params=_cparams(("parallel", "parallel", "arbitrary")), name="hgrn2",
    )(proj, proj, proj, proj, lb.reshape(HG_H, 1, HG_D), nw.reshape(1, HG_D), s0)


def _rwkv_kernel(*refs, c, nt, bb, parts):
    npr = C_RWKV // 256
    p_refs = refs[:npr * bb]
    (sh0_ref, s0_ref, mu_ref, w0_ref, w2_ref, a0_ref, a2_ref, g2_ref, kk_ref, ka_ref, rk_ref,
     lnw_ref, lnb_ref, bd1_ref) = refs[npr * bb:npr * bb + 14]
    o_ref, s_out_ref, sh_out_ref, st_s, carry_s = refs[npr * bb + 14:]
    t = pl.program_id(1)

    @pl.when(t == 0)
    def _():
        st_s[...] = s0_ref[...]
        carry_s[...] = sh0_ref[...]

    B_ = range(bb)
    H_ = range(RW_H)
    ps = [jnp.concatenate([r[...] for r in p_refs[bi * npr:(bi + 1) * npr]], axis=1) for bi in B_]
    xs = []
    for bi in B_:
        p = ps[bi]
        prev = jnp.concatenate([carry_s[bi], p[:-1]], axis=0)
        carry_s[bi] = p[c - 1:c]
        sh_out_ref[bi] = p[c - 1:c]
        xs.append(p + (prev - p) * mu_ref[...])
    o1 = RW_W
    o2 = o1 + LORA_W
    o3 = o2 + RW_W
    o4 = o3 + RW_W
    o5 = o4 + LORA_A
    r = [x[:, :o1] for x in xs]
    k = [x[:, o2:o3] for x in xs]
    v = [x[:, o3:o4] for x in xs]
    wl = [_bdot(jnp.tanh(x[:, o1:o2]), w2_ref[...]) for x in xs]
    al = [_bdot(x[:, o4:o5], a2_ref[...]) for x in xs]
    gate = [_bdot(jax.nn.sigmoid(x[:, o5:]), g2_ref[...]) for x in xs]
    lw = [-jnp.exp(-jax.nn.softplus(-(w0_ref[...] + w)) - 0.5) for w in wl]
    a = [jax.nn.sigmoid(a0_ref[...] + x) for x in al]
    bd1 = bd1_ref[...]
    kk = [x * kk_ref[...] for x in k]
    ssq = [_dotx(x * x, bd1, 2, 1) for x in kk]
    ri = _iota((c, c), 0)
    ci = _iota((c, c), 1)
    incl = ri >= ci
    strict = ri > ci
    tril = jnp.where(incl, 1.0, 0.0).astype(bf16)
    cum = [_dotx(tril, x, 1, 3) for x in lw]
    kk = [x / jnp.maximum(jnp.sqrt(s), 1e-12) for x, s in zip(kk, ssq)]
    k2 = [k[bi] * (1.0 + (a[bi] - 1.0) * ka_ref[...]) for bi in B_]
    beta = [kk[bi] * a[bi] for bi in B_]
    cum_c = [x[c - 1:c, :] for x in cum]
    e_neg = [jnp.exp(-x) for x in cum]
    e_suf = [jnp.exp(cum_c[bi] - cum[bi]) for bi in B_]
    a_bar = [-kk[bi] * jnp.exp(cum[bi] - lw[bi]) for bi in B_]
    r_bar = [r[bi] * jnp.exp(cum[bi]) for bi in B_]
    b_bar = [beta[bi] * e_neg[bi] for bi in B_]
    k_bar = [k2[bi] * e_neg[bi] for bi in B_]
    b_hat = [beta[bi] * e_suf[bi] for bi in B_]
    k_hat = [k2[bi] * e_suf[bi] for bi in B_]

    head = jnp.right_shift(_iota((c, RW_W), 1), 6)
    masks = [head == h for h in H_]
    stack = [jnp.concatenate([jnp.where(m, a_bar[bi], 0.0) for m in masks]
                             + [jnp.where(m, r_bar[bi], 0.0) for m in masks], axis=0).astype(bf16) for bi in B_]
    gb = [lax.dot_general(stack[bi], b_bar[bi].astype(bf16), NT, preferred_element_type=f32) for bi in B_]
    gk = [lax.dot_general(stack[bi], k_bar[bi].astype(bf16), NT, preferred_element_type=f32) for bi in B_]
    eye = jnp.where(ri == ci, 1.0, 0.0)
    vb = [x.astype(bf16) for x in v]
    bh = [(bi, h) for bi in B_ for h in H_]
    blk = lambda g, i: g[i * c:(i + 1) * c]
    lab = {(bi, h): jnp.where(strict, blk(gb[bi], h), 0.0) for bi, h in bh}
    lak = {(bi, h): jnp.where(strict, blk(gk[bi], h), 0.0) for bi, h in bh}
    prb = {(bi, h): jnp.where(incl, blk(gb[bi], RW_H + h), 0.0) for bi, h in bh}
    prk = {(bi, h): jnp.where(incl, blk(gk[bi], RW_H + h), 0.0) for bi, h in bh}
    t1 = {q: _bdot(lak[q], vb[q[0]]) for q in bh}
    y0p = {q: _bdot(prk[q], vb[q[0]]) for q in bh}
    inv = {q: eye + lab[q] for q in bh}
    pw = lab
    for _ in range(int(math.log2(c)) - 1):
        pw = {q: _dotx(pw[q], pw[q], parts, parts) for q in bh}
        inv = {q: inv[q] + _dotx(inv[q], pw[q], parts, parts) for q in bh}
    atp = {q: _dotx(inv[q], a_bar[q[0]], parts, parts) for q in bh}
    u0p = {q: _dotx(inv[q], t1[q], parts, parts) for q in bh}

    def fold(parts_):
        out = []
        for bi in B_:
            acc = jnp.where(masks[0], parts_[(bi, 0)], 0.0)
            for h in range(1, RW_H):
                acc = jnp.where(masks[h], parts_[(bi, h)], acc)
            out.append(acc)
        return out

    a_til = fold(atp)
    u0 = fold(u0p)
    y0 = fold(y0p)
    st = [st_s[bi] for bi in B_]
    stb = [x.astype(bf16) for x in st]
    u = [_bdot(a_til[bi], stb[bi], NT) + u0[bi] for bi in B_]
    yst = [_bdot(r_bar[bi], stb[bi], NT) for bi in B_]
    ub = [x.astype(bf16) for x in u]
    yu = fold({q: _bdot(prb[q], ub[q[0]]) for q in bh})
    upd = [_bdot(jnp.concatenate([u[bi], v[bi]], axis=0), jnp.concatenate([b_hat[bi], k_hat[bi]], axis=0), TN)
           for bi in B_]
    same = jnp.right_shift(_iota((RW_W, RW_W), 0), 6) == jnp.right_shift(_iota((RW_W, RW_W), 1), 6)
    for bi in B_:
        st_s[bi] = st[bi] * jnp.exp(cum_c[bi]) + jnp.where(same, upd[bi], 0.0)

    inv_n = 1.0 / RW_N
    y = [yst[bi] + y0[bi] + yu[bi] for bi in B_]
    mean = [_dotx(x, bd1, 2, 1) * inv_n for x in y]
    bsum = [_dotx(r[bi] * k2[bi] * rk_ref[...], bd1, 2, 1) for bi in B_]
    d = [y[bi] - mean[bi] for bi in B_]
    var = [_dotx(x * x, bd1, 2, 1) * inv_n for x in d]
    for bi in B_:
        yn = d[bi] * lax.rsqrt(var[bi] + RW_GN_EPS) * lnw_ref[...] + lnb_ref[...]
        o_ref[bi] = ((yn + bsum[bi] * v[bi]) * gate[bi]).astype(o_ref.dtype)

    @pl.when(t == nt - 1)
    def _():
        s_out_ref[...] = st_s[...]


def _rwkv(proj, row0, B, T, c, bb, parts, sh0, s0_bd, lw, out_dtype):
    nt = T // c
    rb0 = row0 // c
    pw = 256
    G = B // bb
    vec = lambda n: pl.BlockSpec((1, n), lambda g, t: (0, 0))
    mat = lambda r_, n: pl.BlockSpec((r_, n), lambda g, t: (0, 0))
    p_specs = [pl.BlockSpec((c, pw), lambda g, t, j=j, i=i: (rb0 + (g * bb + i) * nt + t, (CB_RW * LANES) // pw + j))
               for i in range(bb) for j in range(C_RWKV // pw)]
    hh = jnp.arange(RW_W) // RW_N
    bd1 = (hh[:, None] == hh[None, :]).astype(bf16)
    outs = pl.pallas_call(
        functools.partial(_rwkv_kernel, c=c, nt=nt, bb=bb, parts=parts),
        grid=(G, nt),
        in_specs=p_specs + [
            pl.BlockSpec((bb, 1, C_RWKV), lambda g, t: (g, 0, 0)),
            pl.BlockSpec((bb, RW_W, RW_W), lambda g, t: (g, 0, 0)),
            vec(C_RWKV), vec(RW_W), mat(LORA_W, RW_W), vec(RW_W), mat(LORA_A, RW_W), mat(LORA_G, RW_W),
            vec(RW_W), vec(RW_W), vec(RW_W), vec(RW_W), vec(RW_W), mat(RW_W, RW_W)],
        out_specs=[pl.BlockSpec((bb, c, RW_W), lambda g, t: (g, t, 0)),
                   pl.BlockSpec((bb, RW_W, RW_W), lambda g, t: (g, 0, 0)),
                   pl.BlockSpec((bb, 1, C_RWKV), lambda g, t: (g, 0, 0))],
        out_shape=[jax.ShapeDtypeStruct((B, T, RW_W), out_dtype),
                   jax.ShapeDtypeStruct((B, RW_W, RW_W), f32), jax.ShapeDtypeStruct((B, 1, C_RWKV), f32)],
        scratch_shapes=[pltpu.VMEM((bb, RW_W, RW_W), f32), pltpu.VMEM((bb, 1, C_RWKV), f32)],
        compiler_params=_cparams(("parallel", "arbitrary")), name="rwkv7",
    )(*([proj] * (bb * (C_RWKV // pw))), sh0.reshape(B, 1, C_RWKV), s0_bd,
      lw['mu'].reshape(1, -1), lw['w0'].reshape(1, -1), lw['w2'], lw['a0'].reshape(1, -1), lw['a2'], lw['g2'],
      lw['kk'].reshape(1, -1), lw['ka'].reshape(1, -1), lw['rk'].reshape(1, -1),
      lw['lnx_w'].reshape(1, -1), lw['lnx_b'].reshape(1, -1), bd1)
    return outs[0].reshape(B * T, RW_W), outs[1], outs[2]


def _to_blockdiag(s):
    B = s.shape[0]
    eye = jnp.eye(RW_H, dtype=s.dtype)
    return jnp.einsum('bhvk,hg->bhvgk', s, eye).reshape(B, RW_W, RW_W)


def _from_blockdiag(s):
    B = s.shape[0]
    s5 = s.reshape(B, RW_H, RW_N, RW_H, RW_N)
    idx = jnp.arange(RW_H)
    return s5[:, idx, :, idx, :].transpose(1, 0, 2, 3)


def _fox_prep_kernel(q_ref, k_ref, v_ref, qg_ref, kg_ref, qn_ref, kn_ref, kb_ref, vn_ref, vb_ref):
    def nrm(x, g):
        return x * lax.rsqrt(jnp.mean(x * x, axis=-1, keepdims=True) + EPS) * g
    qn_ref[...] = nrm(q_ref[...], qg_ref[...]).astype(qn_ref.dtype)
    kn = nrm(k_ref[...], kg_ref[...])
    kn_ref[...] = kn
    kb_ref[...] = kn.astype(kb_ref.dtype)
    v = v_ref[...]
    vn_ref[...] = v
    vb_ref[...] = v.astype(vb_ref.dtype)


def _fox_prep(proj, row0, rows, tt, qg, kg, lowp):
    rb0 = row0 // tt
    col = lambda cb: pl.BlockSpec((tt, FX_DH), lambda r, h, cb=cb: (rb0 + r, cb + h))
    out = pl.BlockSpec((tt, FX_DH), lambda r, h: (r, h))
    gsp = pl.BlockSpec((1, FX_DH), lambda r, h: (0, 0))
    sds = lambda dt: jax.ShapeDtypeStruct((rows, FX_W), dt)
    return pl.pallas_call(
        _fox_prep_kernel, grid=(rows // tt, FX_H),
        in_specs=[col(CB_FQ), col(CB_FK), col(CB_FV), gsp, gsp],
        out_specs=[out] * 5, out_shape=[sds(lowp), sds(f32), sds(lowp), sds(f32), sds(lowp)],
        compiler_params=_cparams(("parallel", "parallel")), name="fox_prep",
    )(proj, proj, proj, qg.reshape(1, FX_DH), kg.reshape(1, FX_DH))


def _fox_lf_kernel(ff_ref, b_ref, lf_ref, cum_ref, carry_s):
    tt = ff_ref.shape[0]

    @pl.when(pl.program_id(1) == 0)
    def _():
        carry_s[...] = jnp.zeros_like(carry_s)

    lf = jax.nn.log_sigmoid(ff_ref[...] + b_ref[...])
    tril = jnp.where(_iota((tt, tt), 0) >= _iota((tt, tt), 1), 1.0, 0.0).astype(bf16)
    cum = _dotx(tril, lf, 1, 3) + carry_s[...]
    carry_s[...] = cum[tt - 1:tt]
    lf_ref[...] = lf[:, :FX_H]
    cum_ref[...] = cum[:, :FX_H]


def _fox_lf(ffl, row0, B, T, tt, fbias):
    nt = T // tt
    rb0 = row0 // tt
    bias = jnp.zeros((1, LANES), f32).at[0, :FX_H].set(fbias)
    out = pl.BlockSpec((tt, FX_H), lambda b, t: (b * nt + t, 0))
    return pl.pallas_call(
        _fox_lf_kernel, grid=(B, nt),
        in_specs=[pl.BlockSpec((tt, LANES), lambda b, t: (rb0 + b * nt + t, 0)),
                  pl.BlockSpec((1, LANES), lambda b, t: (0, 0))],
        out_specs=[out, out], out_shape=[jax.ShapeDtypeStruct((B * T, FX_H), f32)] * 2,
        scratch_shapes=[pltpu.VMEM((1, LANES), f32)],
        compiler_params=_cparams(("parallel", "arbitrary")), name="fox_logf",
    )(ffl, bias)


def _fox_attn_kernel(q_ref, k_ref, v_ref, cq_ref, ck_ref, o_ref, m_s, l_s, acc_s, *, tq, tk, nk):
    qi = pl.program_id(2)
    ki = pl.program_id(3)

    @pl.when(ki == 0)
    def _():
        m_s[...] = jnp.full_like(m_s, -jnp.inf)
        l_s[...] = jnp.zeros_like(l_s)
        acc_s[...] = jnp.zeros_like(acc_s)

    nh = q_ref.shape[1] // FX_DH

    def block(masked):
        hs = [slice(i * FX_DH, (i + 1) * FX_DH) for i in range(nh)]
        ss = [lax.dot_general(q_ref[:, hs[i]], k_ref[:, hs[i]], NT, preferred_element_type=f32) for i in range(nh)]
        ps = []
        for i in range(nh):
            s = ss[i] * (FX_DH ** -0.5) + (cq_ref[i] - ck_ref[i])
            if masked:
                qpos = qi * tq + _iota((tq, tk), 0)
                kpos = ki * tk + _iota((tq, tk), 1)
                s = jnp.where(kpos <= qpos, s, -jnp.inf)
            m_old = m_s[i]
            m_new = jnp.maximum(m_old, jnp.max(s, axis=-1, keepdims=True))
            alpha = jnp.exp(m_old - m_new)
            p = jnp.exp(s - m_new)
            l_s[i] = alpha * l_s[i] + jnp.sum(p, axis=-1, keepdims=True)
            acc_s[:, hs[i]] = alpha * acc_s[:, hs[i]]
            m_s[i] = m_new
            ps.append(p.astype(bf16))
        for i in range(nh):
            acc_s[:, hs[i]] = acc_s[:, hs[i]] + jnp.dot(ps[i], v_ref[:, hs[i]], preferred_element_type=f32)

    first_q = qi * tq
    first_k = ki * tk
    fully_visible = first_k + (tk - 1) <= first_q
    pl.when(fully_visible)(functools.partial(block, False))
    pl.when(jnp.logical_and(jnp.logical_not(fully_visible), first_k <= first_q + (tq - 1)))(
        functools.partial(block, True))

    @pl.when(ki == nk - 1)
    def _():
        for i in range(nh):
            hs = slice(i * FX_DH, (i + 1) * FX_DH)
            o_ref[:, hs] = (acc_s[:, hs] / l_s[i]).astype(o_ref.dtype)


def _fox_attn(qn, kb, vb, cum, B, T, tq, tk, nh=2):
    nq, nk = T // tq, T // tk
    cq = cum.reshape(B, T, FX_H).transpose(0, 2, 1).reshape(B, FX_H, T, 1)
    ck = cq.reshape(B, FX_H, 1, T)
    last = lambda qi: (qi * tq + tq - 1) // tk
    w = nh * FX_DH
    kv = pl.BlockSpec((tk, w), lambda b, h, qi, ki: (b * nk + jnp.minimum(ki, last(qi)), h))
    return pl.pallas_call(
        functools.partial(_fox_attn_kernel, tq=tq, tk=tk, nk=nk),
        grid=(B, FX_H // nh, nq, nk),
        in_specs=[pl.BlockSpec((tq, w), lambda b, h, qi, ki: (b * nq + qi, h)), kv, kv,
                  pl.BlockSpec((None, nh, tq, 1), lambda b, h, qi, ki: (b, h, qi, 0)),
                  pl.BlockSpec((None, nh, 1, tk), lambda b, h, qi, ki: (b, h, 0, jnp.minimum(ki, last(qi))))],
        out_specs=pl.BlockSpec((tq, w), lambda b, h, qi, ki: (b * nq + qi, h)),
        out_shape=jax.ShapeDtypeStruct((B * T, FX_W), bf16),
        scratch_shapes=[pltpu.VMEM((nh, tq, 1), f32), pltpu.VMEM((nh, tq, 1), f32), pltpu.VMEM((tq, w), f32)],
        compiler_params=_cparams(("parallel", "parallel", "parallel", "arbitrary")), name="fox_attn",
    )(qn, kb, vb, cq, ck)


def _rep_rows(x, td):
    return jnp.concatenate([jnp.broadcast_to(x[h:h + 1], (td, x.shape[1])) for h in range(x.shape[0])], axis=0)


def _fox_dec_kernel(pt_ref, q_ref, kn_ref, vn_ref, cqc_ref, cqr_ref, *refs, td, n_steps, pps):
    kc_refs = refs[:pps]
    vc_refs = refs[pps:2 * pps]
    lft_refs = refs[2 * pps:3 * pps]
    o_ref, m_s, l_s, acc_s, suf_s = refs[3 * pps:]
    j = pl.program_id(1)
    nr = FX_H * td
    scale = FX_DH ** -0.5
    H_ = range(FX_H)
    hcols = [slice(h * FX_DH, (h + 1) * FX_DH) for h in H_]
    hrows = [slice(h * td, (h + 1) * td) for h in H_]

    def heads_q():
        q = q_ref[...]
        return [q[:, hcols[h]].astype(bf16) for h in H_]

    def update(s, v_of):
        m_old = m_s[...]
        m_new = jnp.maximum(m_old, jnp.max(s, axis=-1, keepdims=True))
        alpha = jnp.exp(m_old - m_new)
        p = jnp.exp(s - m_new)
        l_s[...] = alpha * l_s[...] + jnp.sum(p, axis=-1, keepdims=True)
        pv = []
        for h in H_:
            acc = None
            for i in range(s.shape[1] // PAGE):
                ph = p[hrows[h], i * PAGE:(i + 1) * PAGE].astype(bf16)
                t = jnp.dot(ph, v_of(i, h), preferred_element_type=f32)
                acc = t if acc is None else acc + t
            pv.append(acc)
        acc_s[...] = alpha * acc_s[...] + jnp.concatenate(pv, axis=0)
        m_s[...] = m_new

    @pl.when(j == 0)
    def _():
        qh = heads_q()
        pad = jnp.zeros((PAGE - td, FX_W), f32)
        kp = jnp.concatenate([kn_ref[...], pad], axis=0).astype(bf16)
        vp = jnp.concatenate([vn_ref[...], pad], axis=0).astype(bf16)
        s = jnp.concatenate([lax.dot_general(qh[h], kp[:, hcols[h]], NT, preferred_element_type=f32) for h in H_],
                            axis=0) * scale
        s = s + (cqc_ref[...] - _rep_rows(cqr_ref[...], td))
        kpos = _iota((nr, PAGE), 1)
        qpos = jnp.bitwise_and(_iota((nr, PAGE), 0), td - 1)
        s = jnp.where(kpos <= qpos, s, -jnp.inf)
        m_s[...] = jnp.full_like(m_s, -jnp.inf)
        l_s[...] = jnp.zeros_like(l_s)
        acc_s[...] = jnp.zeros_like(acc_s)
        suf_s[...] = jnp.zeros_like(suf_s)
        update(s, lambda i, h: vp[:, hcols[h]])

    @pl.when(j > 0)
    def _():
        later = jnp.where(_iota((PAGE, PAGE), 0) > _iota((PAGE, PAGE), 1), 1.0, 0.0).astype(bf16)
        qh = heads_q()
        cq = cqc_ref[...]
        after = suf_s[...]
        parts = []
        for i in range(pps):
            lft = lft_refs[i][...]
            suf = _dotx(lft, later, 3, 1) + after
            after = after + jnp.sum(lft, axis=-1, keepdims=True)
            s = jnp.concatenate(
                [lax.dot_general(qh[h], kc_refs[i][pl.ds(h, PAGE, stride=FX_H), :].astype(bf16), NT,
                                 preferred_element_type=f32)
                 for h in H_], axis=0) * scale
            parts.append(s + (cq + _rep_rows(suf, td)))
        suf_s[...] = after
        update(jnp.concatenate(parts, axis=1),
               lambda i, h: vc_refs[i][pl.ds(h, PAGE, stride=FX_H), :].astype(bf16))

    @pl.when(j == n_steps - 1)
    def _():
        a = acc_s[...] / l_s[...]
        for h in H_:
            o_ref[:, hcols[h]] = a[hrows[h]].astype(o_ref.dtype)


def _fox_decode(qn, kn, vn, cnew, kcache, vcache, lft, page_table, layer, td):
    Bd, n_pages = page_table.shape
    pps = 4 if n_pages % 4 == 0 else (2 if n_pages % 2 == 0 else 1)
    n_steps = n_pages // pps + 1
    nr = FX_H * td
    c3 = cnew.reshape(Bd, td, FX_H).transpose(0, 2, 1)
    cqc = c3.reshape(Bd, nr, 1)
    cqr = jnp.zeros((Bd, FX_H, PAGE), f32).at[:, :, :td].set(c3)

    def page(i, nd):
        return lambda b, j, pt: (layer, pt[b, n_pages - 1 - i - (jnp.maximum(j, 1) - 1) * pps]) + (0,) * nd

    tok = pl.BlockSpec((td, FX_W), lambda b, j, pt: (b, 0))
    kvs = [pl.BlockSpec((None, None, PAGE * FX_H, FX_DH), page(i, 2)) for i in range(pps)]
    lfs = [pl.BlockSpec((None, None, FX_H, PAGE), page(i, 2)) for i in range(pps)]
    grid_spec = pltpu.PrefetchScalarGridSpec(
        num_scalar_prefetch=1, grid=(Bd, n_steps),
        in_specs=[tok, tok, tok,
                  pl.BlockSpec((None, nr, 1), lambda b, j, pt: (b, 0, 0)),
                  pl.BlockSpec((None, FX_H, PAGE), lambda b, j, pt: (b, 0, 0))] + kvs + kvs + lfs,
        out_specs=pl.BlockSpec((td, FX_W), lambda b, j, pt: (b, 0)),
        scratch_shapes=[pltpu.VMEM((nr, 1), f32), pltpu.VMEM((nr, 1), f32), pltpu.VMEM((nr, FX_DH), f32),
                        pltpu.VMEM((FX_H, PAGE), f32)])
    return pl.pallas_call(
        functools.partial(_fox_dec_kernel, td=td, n_steps=n_steps, pps=pps), grid_spec=grid_spec,
        out_shape=jax.ShapeDtypeStruct((Bd * td, FX_W), f32),
        compiler_params=_cparams(("parallel", "arbitrary")), name="fox_decode",
    )(page_table, qn, kn, vn, cqc, cqr, *([kcache] * pps), *([vcache] * pps), *([lft] * pps))


def _moe_plan_kernel(r_ref, pos_ref, te_ref, cs_s, *, rb, tm_e):
    M = r_ref.shape[0]
    nb = M // rb
    lane = _iota((rb, LANES), 1)
    strict = jnp.where(_iota((rb, rb), 0) > _iota((rb, rb), 1), 1.0, 0.0).astype(bf16)

    def onehots(blk):
        r = r_ref[blk * rb:(blk + 1) * rb, :]
        i1 = r[:, 0:1].astype(i32)
        i2 = r[:, 1:2].astype(i32)
        return lane == i1, lane == i2

    carry = jnp.zeros((1, LANES), f32)
    for blk in range(nb):
        o1, o2 = onehots(blk)
        oh = jnp.where(o1, 1.0, jnp.where(o2, 1.0, 0.0))
        cs_s[blk * rb:(blk + 1) * rb, :] = _dotx(strict, oh.astype(bf16), 1, 1) + carry
        carry = carry + jnp.sum(oh, axis=0, keepdims=True)
    padded = jnp.floor((carry + (tm_e - 1)) * (1.0 / tm_e)) * tm_e
    before = jnp.where(_iota((LANES, LANES), 0) < _iota((LANES, LANES), 1), 1.0, 0.0).astype(bf16)
    off = _dotx(jnp.broadcast_to(padded, (8, LANES)), before, 3, 1)[0:1]
    end = off + padded
    for blk in range(nb):
        o1, o2 = onehots(blk)
        base = off + cs_s[blk * rb:(blk + 1) * rb, :]
        p1 = jnp.sum(jnp.where(o1, base, 0.0), axis=-1, keepdims=True)
        p2 = jnp.sum(jnp.where(o2, base, 0.0), axis=-1, keepdims=True)
        pos_ref[blk * rb:(blk + 1) * rb, :] = jnp.where(lane == 0, p1, jnp.where(lane == 1, p2, 0.0)).astype(i32)
    l1 = _iota((1, LANES), 1)
    start = (l1 * tm_e).astype(f32)
    te = jnp.zeros((1, LANES), f32)
    for e in range(N_EXPERTS):
        end_e = jnp.sum(jnp.where(l1 == e, end, 0.0), axis=-1, keepdims=True)
        te = te + jnp.where(end_e <= start, 1.0, 0.0)
    total = jnp.sum(jnp.where(l1 == N_EXPERTS - 1, end, 0.0), axis=-1, keepdims=True)
    te = jnp.minimum(te, N_EXPERTS - 1.0)
    te_ref[...] = jnp.where(l1 == LANES - 1, total * (1.0 / tm_e), te).astype(i32)


def _moe_plan(route, tm_e, rb):
    M = route.shape[0]
    return pl.pallas_call(
        functools.partial(_moe_plan_kernel, rb=rb, tm_e=tm_e),
        out_shape=[jax.ShapeDtypeStruct((M, LANES), i32), jax.ShapeDtypeStruct((1, LANES), i32)],
        scratch_shapes=[pltpu.VMEM((M, LANES), f32)],
        compiler_params=pltpu.CompilerParams(vmem_limit_bytes=VMEM_LIMIT_BYTES), name="moe_plan")(route)


def _moe_gather_kernel(pos_ref, te_ref, u_ref, xs_ref, src_s, buf, sem, *, tm_e, n_pairs):
    t = pl.program_id(0)
    rows = src_s.shape[0]

    @pl.when(t == 0)
    def _():
        def clear(i, c):
            src_s[i] = 0
            return c

        def put(i, c):
            src_s[pos_ref[i]] = jnp.right_shift(i, 1)
            return c

        lax.fori_loop(0, rows, clear, 0, unroll=8)
        lax.fori_loop(0, n_pairs, put, 0, unroll=8)

    ns = u_ref.shape[1]

    def copy(r):
        return pltpu.make_async_copy(u_ref.at[src_s[t * tm_e + r]], buf.at[pl.ds(pl.multiple_of(r * ns, ns), ns)], sem)

    @pl.when(t < te_ref[LANES - 1])
    def _():
        def start(r, c):
            copy(r).start()
            return c

        def wait(r, c):
            copy(r).wait()
            return c

        lax.fori_loop(0, tm_e, start, 0)
        lax.fori_loop(0, tm_e, wait, 0)
        for s in range(ns):
            xs_ref[:, s * LANES:(s + 1) * LANES] = buf[pl.ds(s, tm_e, stride=ns), :].astype(xs_ref.dtype)

    @pl.when(t >= te_ref[LANES - 1])
    def _():
        xs_ref[...] = jnp.zeros_like(xs_ref)


def _moe_gather(u3, pos_flat, te, rows, tm_e):
    M, S, _ = u3.shape
    grid_spec = pltpu.PrefetchScalarGridSpec(
        num_scalar_prefetch=2, grid=(rows // tm_e,),
        in_specs=[pl.BlockSpec(memory_space=pl.ANY)],
        out_specs=pl.BlockSpec((tm_e, S * LANES), lambda t, p, te_: (t, 0)),
        scratch_shapes=[pltpu.SMEM((rows,), i32), pltpu.VMEM((tm_e * S, LANES), f32), pltpu.SemaphoreType.DMA(())])
    return pl.pallas_call(
        functools.partial(_moe_gather_kernel, tm_e=tm_e, n_pairs=2 * M), grid_spec=grid_spec,
        out_shape=jax.ShapeDtypeStruct((rows, S * LANES), bf16),
        compiler_params=_cparams(("arbitrary",)), name="moe_gather")(pos_flat, te, u3)


def _moe_mm_kernel(te_ref, *refs, glu):
    it = iter(refs)
    a_ref = next(it)
    w_ref = next(it)
    w3_ref = next(it) if glu else None
    o_ref = next(it)
    wb_ref = next(it)
    wb3_ref = next(it) if glu else None
    t = pl.program_id(1)
    e = te_ref[t]
    e_prev = te_ref[jnp.maximum(t - 1, 0)]

    @pl.when((t == 0) | (e != e_prev))
    def _():
        wb_ref[...] = w_ref[...].astype(bf16)
        if glu:
            wb3_ref[...] = w3_ref[...].astype(bf16)

    @pl.when(t < te_ref[LANES - 1])
    def _():
        a = a_ref[...]
        y = jnp.dot(a, wb_ref[...], preferred_element_type=f32)
        if glu:
            y = jax.nn.silu(y) * jnp.dot(a, wb3_ref[...], preferred_element_type=f32)
        if len(o_ref.shape) == 3:
            for s in range(o_ref.shape[1]):
                o_ref[:, s, :] = y[:, s * LANES:(s + 1) * LANES].astype(o_ref.dtype)
        else:
            o_ref[...] = y.astype(o_ref.dtype)

    @pl.when(t >= te_ref[LANES - 1])
    def _():
        o_ref[...] = jnp.zeros_like(o_ref)


def _moe_matmul(a, te, w, fi, tm_e, tn, w3=None, out_dtype=f32, slabs=False, name="moe_matmul"):
    R, K = a.shape
    N = w.shape[-1]
    glu = w3 is not None
    w_spec = pl.BlockSpec((None, None, K, tn), lambda n, t, te_: (fi, te_[t], 0, n))
    in_specs = [pl.BlockSpec((tm_e, K), lambda n, t, te_: (t, 0)), w_spec] + ([w_spec] if glu else [])
    args = [a, w] + ([w3] if glu else [])
    scratch = [pltpu.VMEM((K, tn), bf16)] * (2 if glu else 1)
    if slabs:
        out_shape = jax.ShapeDtypeStruct((R, N // tn, tn // LANES, LANES), out_dtype)
        out_spec = pl.BlockSpec((tm_e, None, tn // LANES, LANES), lambda n, t, te_: (t, n, 0, 0))
    else:
        out_shape = jax.ShapeDtypeStruct((R, N), out_dtype)
        out_spec = pl.BlockSpec((tm_e, tn), lambda n, t, te_: (t, n))
    grid_spec = pltpu.PrefetchScalarGridSpec(
        num_scalar_prefetch=1, grid=(N // tn, R // tm_e), in_specs=in_specs,
        out_specs=out_spec, scratch_shapes=scratch)
    return pl.pallas_call(
        functools.partial(_moe_mm_kernel, glu=glu), grid_spec=grid_spec, out_shape=out_shape,
        compiler_params=_cparams(("arbitrary", "arbitrary")), name=name)(te, *args)


def _moe_combine_kernel(pos_ref, h_ref, r_ref, y_ref, o_ref, buf0, buf1, sem, *, tb):
    i0 = pl.program_id(0) * tb
    bufs = (buf0, buf1)
    ns = y_ref.shape[1]

    def copy(i, j):
        return pltpu.make_async_copy(y_ref.at[pos_ref[2 * (i0 + i) + j]],
                                     bufs[j].at[pl.ds(pl.multiple_of(i * ns, ns), ns)], sem)

    def start(i, c):
        copy(i, 0).start()
        copy(i, 1).start()
        return c

    def wait(i, c):
        copy(i, 0).wait()
        copy(i, 1).wait()
        return c

    lax.fori_loop(0, tb, start, 0)
    lax.fori_loop(0, tb, wait, 0)
    r = r_ref[...]
    g1 = r[:, 2:3]
    g2 = r[:, 3:4]
    for s in range(ns):
        c0 = s * LANES
        y0 = buf0[pl.ds(s, tb, stride=ns), :]
        y1 = buf1[pl.ds(s, tb, stride=ns), :]
        o_ref[:, c0:c0 + LANES] = h_ref[:, c0:c0 + LANES] + (g1 * y0 + g2 * y1)


def _moe_combine(h, route, y3, pos_flat, tb):
    M, D = h.shape
    grid_spec = pltpu.PrefetchScalarGridSpec(
        num_scalar_prefetch=1, grid=(M // tb,),
        in_specs=[pl.BlockSpec((tb, D), lambda m, p: (m, 0)), pl.BlockSpec((tb, LANES), lambda m, p: (m, 0)),
                  pl.BlockSpec(memory_space=pl.ANY)],
        out_specs=pl.BlockSpec((tb, D), lambda m, p: (m, 0)),
        scratch_shapes=[pltpu.VMEM((tb * y3.shape[1], LANES), f32), pltpu.VMEM((tb * y3.shape[1], LANES), f32),
                        pltpu.SemaphoreType.DMA(())])
    return pl.pallas_call(
        functools.partial(_moe_combine_kernel, tb=tb), grid_spec=grid_spec,
        out_shape=jax.ShapeDtypeStruct((M, D), f32),
        compiler_params=_cparams(("arbitrary",)), name="moe_combine")(pos_flat, h, route, y3)


def _moe(h, u3, route, w1, w3, w2, fi, tm, tm_e=256):
    M = h.shape[0]
    rows = (pl.cdiv(2 * M, tm_e) + N_EXPERTS) * tm_e
    pos, te = _moe_plan(route, tm_e, tm)
    pos_flat = pos[:, :2].reshape(-1)
    te = te.reshape(-1)
    xs = _moe_gather(u3, pos_flat, te, rows, tm_e)
    act = _moe_matmul(xs, te, w1, fi, tm_e, 1024, w3=w3, out_dtype=bf16, name="moe_up")
    y = _moe_matmul(act, te, w2, fi, tm_e, 512, slabs=True, name="moe_down")
    y3 = y.reshape(rows, h.shape[1] // LANES, LANES)
    return _moe_combine(h, route, y3, pos_flat, tm // 2)


def kernel(x_prompt, x_sample, cache_fox_k, cache_fox_v, cache_fox_logf, page_table, state_hgrn, state_rwkv, state_rwkv_shift, norm_mix, w_in, w_out, hgrn_lb, hgrn_norm, rwkv_mu, rwkv_w0, rwkv_w2, rwkv_a0, rwkv_a2, rwkv_g2, rwkv_kk, rwkv_ka, rwkv_rk, rwkv_lnx_w, rwkv_lnx_b, fox_qnorm, fox_knorm, fox_fbias, norm_ffn, ffn_w1, ffn_w3, ffn_w2, moe_router, moe_w1, moe_w3, moe_w2):
    B, T, D = x_prompt.shape
    Bd, Td, _ = x_sample.shape
    depth = w_in.shape[0]
    n_pool = cache_fox_k.shape[1]
    Mp, Ms = B * T, Bd * Td
    M = Mp + Ms
    tm = 688 if M % 688 == 0 else M
    d_ff = ffn_w1.shape[-1]

    h = jnp.concatenate([x_prompt.reshape(Mp, D), x_sample.reshape(Ms, D)], axis=0)
    lb_all = jnp.cumsum(jax.nn.softmax(hgrn_lb.astype(f32), axis=0), axis=0)
    lb_all = lb_all - lb_all[:1]
    kcache = cache_fox_k.reshape(depth, n_pool, PAGE * FX_H, FX_DH)
    vcache = cache_fox_v.reshape(depth, n_pool, PAGE * FX_H, FX_DH)
    lft = cache_fox_logf.transpose(0, 1, 3, 2)
    zeros_hg = jnp.zeros((B, HG_H, HG_D, HG_D), f32)
    zeros_rw = jnp.zeros((B, RW_W, RW_W), f32)
    zeros_sh = jnp.zeros((B, C_RWKV), f32)
    c_p = 64 if T % 64 == 0 else T
    tb_p = 256 if T % 256 == 0 else T
    tt_p = 512 if T % 512 == 0 else T

    outs = [[] for _ in range(12)]
    for l in range(depth):
        fi = l // 2
        rw = dict(mu=rwkv_mu[l], w0=rwkv_w0[l], w2=rwkv_w2[l], a0=rwkv_a0[l], a2=rwkv_a2[l], g2=rwkv_g2[l],
                  kk=rwkv_kk[l], ka=rwkv_ka[l], rk=rwkv_rk[l], lnx_w=rwkv_lnx_w[l], lnx_b=rwkv_lnx_b[l])
        u, ffl = _rmsnorm(h, norm_mix[l], w_small=w_in[l][:, N_PROJ:], tm=tm)
        proj = _matmul(u, w_in, l, N_PROJ, 768, tm, name="mm_in")

        o_hg, s_hg = _hgrn(proj, 0, B, T, lb_all[l], hgrn_norm[l], zeros_hg, 16 if tb_p % 16 == 0 else tb_p, tb_p, bf16)
        o_rw, s_rw, sh = _rwkv(proj, 0, B, T, c_p, B, 1, zeros_sh, zeros_rw, rw, bf16)
        qn, kn, kb, vn, vb = _fox_prep(proj, 0, Mp, min(1024, Mp), fox_qnorm[l], fox_knorm[l], bf16)
        lf, cum = _fox_lf(ffl, 0, B, T, tt_p, fox_fbias[l])
        o_fx = _fox_attn(qn, kb, vb, cum, B, T, tt_p, tt_p)
        for i, x in enumerate((kn.reshape(B, T, FX_H, FX_DH), vn.reshape(B, T, FX_H, FX_DH),
                               lf.reshape(B, T, FX_H), s_hg, _from_blockdiag(s_rw), sh.reshape(B, C_RWKV))):
            outs[i].append(x)

        o_hg_s, s_hg_s = _hgrn(proj, Mp, Bd, Td, lb_all[l], hgrn_norm[l], state_hgrn[l], Td, Td, f32)
        o_rw_s, s_rw_s, sh_s = _rwkv(proj, Mp, Bd, Td, Td, 1, 2, state_rwkv_shift[l], _to_blockdiag(state_rwkv[l]),
                                     rw, f32)
        qn_s, kn_s, _, vn_s, _ = _fox_prep(proj, Mp, Ms, Ms, fox_qnorm[l], fox_knorm[l], f32)
        lf_s, cnew = _fox_lf(ffl, Mp, Bd, Td, Td, fox_fbias[l])
        o_fx_s = _fox_decode(qn_s, kn_s, vn_s, cnew, kcache, vcache, lft, page_table, l, Td)
        for i, x in enumerate((kn_s.reshape(Bd, Td, FX_H, FX_DH), vn_s.reshape(Bd, Td, FX_H, FX_DH),
                               lf_s.reshape(Bd, Td, FX_H), s_hg_s, _from_blockdiag(s_rw_s), sh_s.reshape(Bd, C_RWKV))):
            outs[6 + i].append(x)

        h = _mm_out(h, [o_hg, o_rw, o_fx], 0, min(512, Mp), w_out, l)
        h = _mm_out(h, [o_hg_s, o_rw_s, o_fx_s], Mp, Ms, w_out, l)
        if l % 2 == 0:
            u2 = _rmsnorm(h, norm_ffn[l], tm=tm)
            act = _matmul(u2, ffn_w1, fi, d_ff, 512, tm, w3=ffn_w3, out_dtype=bf16, name="ffn_up")
            h = _matmul(act, ffn_w2, fi, D, 512, tm, res=h, name="ffn_down")
        else:
            u3, route = _rmsnorm(h, norm_ffn[l], w_small=moe_router[fi], route=True, tm=tm, slabs=True)
            h = _moe(h, u3, route, moe_w1, moe_w3, moe_w2, fi, tm)

    st = [jnp.stack(o) for o in outs]
    return (h[:Mp].reshape(B, T, D), h[Mp:].reshape(Bd, Td, D), *st)
```

```python
import functools
import math

import jax
import jax.numpy as jnp
from jax import lax
from jax.experimental import pallas as pl
from jax.experimental.pallas import tpu as pltpu

f32 = jnp.float32
bf16 = jnp.bfloat16
i32 = jnp.int32

D_MODEL = 2048
DEPTH = 4
PAGE = 128
HG_H, HG_D = 4, 128
HG_W = HG_H * HG_D
RW_H, RW_N = 8, 64
RW_W = RW_H * RW_N
LORA_W, LORA_A, LORA_G = 64, 64, 128
C_RWKV = 3 * RW_W + LORA_W + LORA_A + LORA_G
FX_H, FX_DH = 8, 128
FX_W = FX_H * FX_DH
N_EXPERTS = 8
EPS = 1e-6
RW_GN_EPS = 64e-5
CB_HG = 0
CB_RW = (4 * HG_W) // 128
CB_FQ = CB_RW + C_RWKV // 128
CB_FK = CB_FQ + FX_H
CB_FV = CB_FK + FX_H
N_PROJ = (CB_FV + FX_H) * 128

LANES = 128
VMEM_LIMIT_BYTES = 56 * 1024 * 1024

NT = (((1,), (1,)), ((), ()))
TN = (((0,), (0,)), ((), ()))


def _cparams(sem):
    return pltpu.CompilerParams(dimension_semantics=sem, vmem_limit_bytes=VMEM_LIMIT_BYTES)


def _split(x, n):
    parts, r = [], x
    for i in range(n):
        p = r.astype(bf16)
        parts.append(p)
        if i + 1 < n:
            r = r - p.astype(f32)
    return parts


def _dotx(a, b, na, nb, dims=None):
    ap = _split(a, na) if a.dtype != bf16 else [a]
    bp = _split(b, nb) if b.dtype != bf16 else [b]
    n = max(len(ap), len(bp))
    acc = None
    for i, x in enumerate(ap):
        for j, y in enumerate(bp):
            if i + j < n:
                t = (jnp.dot(x, y, preferred_element_type=f32) if dims is None
                     else lax.dot_general(x, y, dims, preferred_element_type=f32))
                acc = t if acc is None else acc + t
    return acc


def _bdot(a, b, dims=None):
    return _dotx(a.astype(bf16), b.astype(bf16), 1, 1, dims)


def _iota(shape, axis):
    return lax.broadcasted_iota(i32, shape, axis)


def _route(lg):
    lane = _iota(lg.shape, 1)
    l1 = jnp.where(lane < N_EXPERTS, lg, -jnp.inf)
    m1 = jnp.max(l1, axis=-1, keepdims=True)
    i1 = jnp.min(jnp.where(l1 == m1, lane, LANES), axis=-1, keepdims=True)
    l2 = jnp.where(lane == i1, -jnp.inf, l1)
    m2 = jnp.max(l2, axis=-1, keepdims=True)
    i2 = jnp.min(jnp.where(l2 == m2, lane, LANES), axis=-1, keepdims=True)
    e = jnp.exp(m2 - m1)
    den = 1.0 + e
    g1 = 1.0 / den
    g2 = e / den
    return jnp.where(lane == 0, i1.astype(f32),
                     jnp.where(lane == 1, i2.astype(f32),
                               jnp.where(lane == 2, g1, jnp.where(lane == 3, g2, 0.0))))


def _norm_kernel(*refs, has_small, route):
    if has_small:
        x_ref, g_ref, ws_ref, u_ref, s_ref = refs
    else:
        x_ref, g_ref, u_ref = refs
    x = x_ref[...]
    ms = jnp.mean(x * x, axis=-1, keepdims=True)
    u = x * lax.rsqrt(ms + EPS) * g_ref[...]
    if u_ref.dtype == f32:
        ns = u.shape[1] // LANES
        for s in range(ns):
            u_ref[pl.ds(s, u.shape[0], stride=ns), :] = u[:, s * LANES:(s + 1) * LANES]
    else:
        u_ref[...] = u.astype(bf16)
    if has_small:
        s = _dotx(u, ws_ref[...], 2, 2)
        s_ref[...] = _route(s) if route else s


def _rmsnorm(x, g, w_small=None, route=False, tm=688, slabs=False):
    M, D = x.shape
    has_small = w_small is not None
    in_specs = [pl.BlockSpec((tm, D), lambda m: (m, 0)), pl.BlockSpec((1, D), lambda m: (0, 0))]
    args = [x, g.reshape(1, D)]
    if slabs:
        out_shape = [jax.ShapeDtypeStruct((M * (D // LANES), LANES), f32)]
        out_specs = [pl.BlockSpec((tm * (D // LANES), LANES), lambda m: (m, 0))]
    else:
        out_shape = [jax.ShapeDtypeStruct((M, D), bf16)]
        out_specs = [pl.BlockSpec((tm, D), lambda m: (m, 0))]
    if has_small:
        ws = jnp.zeros((D, LANES), f32).at[:, :w_small.shape[1]].set(w_small)
        in_specs.append(pl.BlockSpec((D, LANES), lambda m: (0, 0)))
        args.append(ws)
        out_shape.append(jax.ShapeDtypeStruct((M, LANES), f32))
        out_specs.append(pl.BlockSpec((tm, LANES), lambda m: (m, 0)))
    outs = pl.pallas_call(
        functools.partial(_norm_kernel, has_small=has_small, route=route),
        grid=(M // tm,), in_specs=in_specs, out_specs=out_specs, out_shape=out_shape,
        compiler_params=_cparams(("parallel",)), name="rmsnorm")(*args)
    outs = list(outs)
    if slabs:
        outs[0] = outs[0].reshape(M, D // LANES, LANES)
    return outs if has_small else outs[0]


def _mm_kernel(*refs, glu, has_res):
    it = iter(refs)
    a_ref = next(it)
    w_ref = next(it)
    w3_ref = next(it) if glu else None
    r_ref = next(it) if has_res else None
    o_ref = next(it)
    wb_ref = next(it)
    wb3_ref = next(it) if glu else None

    @pl.when(pl.program_id(1) == 0)
    def _():
        wb_ref[...] = w_ref[...].astype(bf16)
        if glu:
            wb3_ref[...] = w3_ref[...].astype(bf16)

    a = a_ref[...]
    y = jnp.dot(a, wb_ref[...], preferred_element_type=f32)
    if glu:
        y = jax.nn.silu(y) * jnp.dot(a, wb3_ref[...], preferred_element_type=f32)
    if has_res:
        y = r_ref[...] + y
    o_ref[...] = y.astype(o_ref.dtype)


def _matmul(a, w, lidx, n_out, tn, tm, w3=None, res=None, out_dtype=f32, name="matmul"):
    M, K = a.shape
    glu = w3 is not None
    has_res = res is not None
    w_spec = pl.BlockSpec((None, K, tn), lambda n, m: (lidx, 0, n))
    in_specs = [pl.BlockSpec((tm, K), lambda n, m: (m, 0)), w_spec]
    args = [a, w]
    scratch = [pltpu.VMEM((K, tn), bf16)]
    if glu:
        in_specs.append(w_spec)
        args.append(w3)
        scratch.append(pltpu.VMEM((K, tn), bf16))
    if has_res:
        in_specs.append(pl.BlockSpec((tm, tn), lambda n, m: (m, n)))
        args.append(res)
    return pl.pallas_call(
        functools.partial(_mm_kernel, glu=glu, has_res=has_res),
        grid=(pl.cdiv(n_out, tn), M // tm), in_specs=in_specs,
        out_specs=pl.BlockSpec((tm, tn), lambda n, m: (m, n)),
        out_shape=jax.ShapeDtypeStruct((M, n_out), out_dtype), scratch_shapes=scratch,
        compiler_params=_cparams(("arbitrary", "arbitrary")), name=name)(*args)


def _mm_out_kernel(*refs, n_a):
    a_refs = refs[:n_a]
    w_ref, h_ref, o_ref, wb_ref = refs[n_a:]

    @pl.when(pl.program_id(1) == 0)
    def _():
        wb_ref[...] = w_ref[...].astype(bf16)

    acc = h_ref[...]
    k0 = 0
    for a_ref in a_refs:
        kw = a_ref.shape[1]
        acc = acc + jnp.dot(a_ref[...].astype(bf16), wb_ref[k0:k0 + kw, :], preferred_element_type=f32)
        k0 += kw
    o_ref[...] = acc


def _mm_out(h, parts, row0, tm, w, lidx, tn=512):
    rows = parts[0].shape[0]
    K, N = w.shape[1], w.shape[2]
    rb0 = row0 // tm
    tile = pl.BlockSpec((tm, tn), lambda n, m: (rb0 + m, n))
    in_specs = [pl.BlockSpec((tm, a.shape[1]), lambda n, m: (m, 0)) for a in parts]
    in_specs += [pl.BlockSpec((None, K, tn), lambda n, m: (lidx, 0, n)), tile]
    return pl.pallas_call(
        functools.partial(_mm_out_kernel, n_a=len(parts)),
        grid=(N // tn, rows // tm), in_specs=in_specs, out_specs=tile,
        out_shape=jax.ShapeDtypeStruct(h.shape, h.dtype), scratch_shapes=[pltpu.VMEM((K, tn), bf16)],
        input_output_aliases={len(parts) + 1: 0},
        compiler_params=_cparams(("arbitrary", "arbitrary")), name="mm_out")(*parts, w, h)


def _hgrn_kernel(q_ref, f_ref, i_ref, g_ref, lb_ref, nw_ref, s0_ref, o_ref, s_out_ref,
                 st_s, qh_s, kh_s, gr_s, o_s, *, c, nt):
    t = pl.program_id(2)
    tb = q_ref.shape[0]
    lc = int(math.log2(c))

    @pl.when(t == 0)
    def _():
        st_s[...] = s0_ref[...].T

    lb = lb_ref[...]
    fl = f_ref[...]
    lf = jnp.logaddexp(jnp.log(lb), jnp.log1p(-lb) + jax.nn.log_sigmoid(fl))
    kh_s[...] = (1.0 - lb) * jax.nn.sigmoid(-fl)
    qh_s[...] = jax.nn.silu(q_ref[...])
    row = _iota((tb, tb), 0)
    col = _iota((tb, tb), 1)
    same = jnp.right_shift(row, lc) == jnp.right_shift(col, lc)
    bd = jnp.where(same, jnp.where(col <= row, 1.0, 0.0), 0.0).astype(bf16)
    gr_s[...] = _dotx(bd, lf, 1, 3)
    ones = jnp.ones((HG_D, HG_D), bf16)
    rows = _iota((c, HG_D), 0)

    nsb = tb // c
    incs, decs = [], []
    for j in range(nsb):
        r0 = j * c
        qs = qh_s[pl.ds(r0, c), :]
        ks = kh_s[pl.ds(r0, c), :]
        gs = gr_s[pl.ds(r0, c), :]
        vs = i_ref[pl.ds(r0, c), :]
        glast = gr_s[pl.ds(r0 + (c - 1), 1), :]
        xs = []
        for s in range(c):
            m = rows >= s
            e = jnp.exp(jnp.where(m, gs - gs[s:s + 1, :], 0.0))
            xs.append(jnp.where(m, qs * ks[s:s + 1, :] * e, 0.0))
        p = _dotx(jnp.concatenate(xs, axis=0), ones, 2, 1)
        o = p[0:c, :] * vs[0:1, :]
        for s in range(1, c):
            o = o + p[s * c:(s + 1) * c, :] * vs[s:s + 1, :]
        o_s[pl.ds(r0, c), :] = o
        incs.append(_bdot(vs, ks * jnp.exp(glast - gs), TN))
        decs.append(jnp.exp(glast))
    st = st_s[...]
    for j in range(nsb):
        r0 = j * c
        qt = qh_s[pl.ds(r0, c), :] * jnp.exp(gr_s[pl.ds(r0, c), :])
        o_s[pl.ds(r0, c), :] = o_s[pl.ds(r0, c), :] + _bdot(qt, st, NT)
        st = st * decs[j] + incs[j]

    st_s[...] = st
    o = o_s[...]
    y = o * lax.rsqrt(jnp.mean(o * o, axis=-1, keepdims=True) + EPS) * nw_ref[...]
    o_ref[...] = (y * jax.nn.silu(g_ref[...])).astype(o_ref.dtype)

    @pl.when(t == nt - 1)
    def _():
        s_out_ref[...] = st_s[...].T


def _hgrn(proj, row0, B, T, lb, nw, s0, c, tb, out_dtype):
    nt = T // tb
    rb0 = row0 // tb

    def col(k):
        return pl.BlockSpec((tb, HG_D), lambda b, h, t, k=k: (rb0 + b * nt + t, CB_HG + k * HG_H + h))

    st_spec = pl.BlockSpec((None, None, HG_D, HG_D), lambda b, h, t: (b, h, 0, 0))
    return pl.pallas_call(
        functools.partial(_hgrn_kernel, c=c, nt=nt),
        grid=(B, HG_H, nt),
        in_specs=[col(0), col(1), col(2), col(3),
                  pl.BlockSpec((None, 1, HG_D), lambda b, h, t: (h, 0, 0)),
                  pl.BlockSpec((1, HG_D), lambda b, h, t: (0, 0)),
                  st_spec],
        out_specs=[pl.BlockSpec((tb, HG_D), lambda b, h, t: (b * nt + t, h)), st_spec],
        out_shape=[jax.ShapeDtypeStruct((B * T, HG_W), out_dtype),
                   jax.ShapeDtypeStruct((B, HG_H, HG_D, HG_D), f32)],
        scratch_shapes=[pltpu.VMEM((HG_D, HG_D), f32)] + [pltpu.VMEM((tb, HG_D), f32)] * 4,
        compiler_params=_cparams(("parallel", "parallel", "arbitrary")), name="hgrn2",
    )(proj, proj, proj, proj, lb.reshape(HG_H, 1, HG_D), nw.reshape(1, HG_D), s0)


def _rwkv_kernel(*refs, c, nt, bb, parts):
    npr = C_RWKV // 256
    p_refs = refs[:npr * bb]
    (sh0_ref, s0_ref, mu_ref, w0_ref, w2_ref, a0_ref, a2_ref, g2_ref, kk_ref, ka_ref, rk_ref,
     lnw_ref, lnb_ref, bd1_ref) = refs[npr * bb:npr * bb + 14]
    o_ref, s_out_ref, sh_out_ref, st_s, carry_s = refs[npr * bb + 14:]
    t = pl.program_id(1)

    @pl.when(t == 0)
    def _():
        st_s[...] = s0_ref[...]
        carry_s[...] = sh0_ref[...]

    B_ = range(bb)
    H_ = range(RW_H)
    ps = [jnp.concatenate([r[...] for r in p_refs[bi * npr:(bi + 1) * npr]], axis=1) for bi in B_]
    xs = []
    for bi in B_:
        p = ps[bi]
        prev = jnp.concatenate([carry_s[bi], p[:-1]], axis=0)
        carry_s[bi] = p[c - 1:c]
        sh_out_ref[bi] = p[c - 1:c]
        xs.append(p + (prev - p) * mu_ref[...])
    o1 = RW_W
    o2 = o1 + LORA_W
    o3 = o2 + RW_W
    o4 = o3 + RW_W
    o5 = o4 + LORA_A
    r = [x[:, :o1] for x in xs]
    k = [x[:, o2:o3] for x in xs]
    v = [x[:, o3:o4] for x in xs]
    wl = [_bdot(jnp.tanh(x[:, o1:o2]), w2_ref[...]) for x in xs]
    al = [_bdot(x[:, o4:o5], a2_ref[...]) for x in xs]
    gate = [_bdot(jax.nn.sigmoid(x[:, o5:]), g2_ref[...]) for x in xs]
    lw = [-jnp.exp(-jax.nn.softplus(-(w0_ref[...] + w)) - 0.5) for w in wl]
    a = [jax.nn.sigmoid(a0_ref[...] + x) for x in al]
    bd1 = bd1_ref[...]
    kk = [x * kk_ref[...] for x in k]
    ssq = [_dotx(x * x, bd1, 2, 1) for x in kk]
    ri = _iota((c, c), 0)
    ci = _iota((c, c), 1)
    incl = ri >= ci
    strict = ri > ci
    tril = jnp.where(incl, 1.0, 0.0).astype(bf16)
    cum = [_dotx(tril, x, 1, 3) for x in lw]
    kk = [x / jnp.maximum(jnp.sqrt(s), 1e-12) for x, s in zip(kk, ssq)]
    k2 = [k[bi] * (1.0 + (a[bi] - 1.0) * ka_ref[...]) for bi in B_]
    beta = [kk[bi] * a[bi] for bi in B_]
    cum_c = [x[c - 1:c, :] for x in cum]
    e_neg = [jnp.exp(-x) for x in cum]
    e_suf = [jnp.exp(cum_c[bi] - cum[bi]) for bi in B_]
    a_bar = [-kk[bi] * jnp.exp(cum[bi] - lw[bi]) for bi in B_]
    r_bar = [r[bi] * jnp.exp(cum[bi]) for bi in B_]
    b_bar = [beta[bi] * e_neg[bi] for bi in B_]
    k_bar = [k2[bi] * e_neg[bi] for bi in B_]
    b_hat = [beta[bi] * e_suf[bi] for bi in B_]
    k_hat = [k2[bi] * e_suf[bi] for bi in B_]

    head = jnp.right_shift(_iota((c, RW_W), 1), 6)
    masks = [head == h for h in H_]
    stack = [jnp.concatenate([jnp.where(m, a_bar[bi], 0.0) for m in masks]
                             + [jnp.where(m, r_bar[bi], 0.0) for m in masks], axis=0).astype(bf16) for bi in B_]
    gb = [lax.dot_general(stack[bi], b_bar[bi].astype(bf16), NT, preferred_element_type=f32) for bi in B_]
    gk = [lax.dot_general(stack[bi], k_bar[bi].astype(bf16), NT, preferred_element_type=f32) for bi in B_]
    eye = jnp.where(ri == ci, 1.0, 0.0)
    vb = [x.astype(bf16) for x in v]
    bh = [(bi, h) for bi in B_ for h in H_]
    blk = lambda g, i: g[i * c:(i + 1) * c]
    lab = {(bi, h): jnp.where(strict, blk(gb[bi], h), 0.0) for bi, h in bh}
    lak = {(bi, h): jnp.where(strict, blk(gk[bi], h), 0.0) for bi, h in bh}
    prb = {(bi, h): jnp.where(incl, blk(gb[bi], RW_H + h), 0.0) for bi, h in bh}
    prk = {(bi, h): jnp.where(incl, blk(gk[bi], RW_H + h), 0.0) for bi, h in bh}
    t1 = {q: _bdot(lak[q], vb[q[0]]) for q in bh}
    y0p = {q: _bdot(prk[q], vb[q[0]]) for q in bh}
    inv = {q: eye + lab[q] for q in bh}
    pw = lab
    for _ in range(int(math.log2(c)) - 1):
        pw = {q: _dotx(pw[q], pw[q], parts, parts) for q in bh}
        inv = {q: inv[q] + _dotx(inv[q], pw[q], parts, parts) for q in bh}
    atp = {q: _dotx(inv[q], a_bar[q[0]], parts, parts) for q in bh}
    u0p = {q: _dotx(inv[q], t1[q], parts, parts) for q in bh}

    def fold(parts_):
        out = []
        for bi in B_:
            acc = jnp.where(masks[0], parts_[(bi, 0)], 0.0)
            for h in range(1, RW_H):
                acc = jnp.where(masks[h], parts_[(bi, h)], acc)
            out.append(acc)
        return out

    a_til = fold(atp)
    u0 = fold(u0p)
    y0 = fold(y0p)
    st = [st_s[bi] for bi in B_]
    stb = [x.astype(bf16) for x in st]
    u = [_bdot(a_til[bi], stb[bi], NT) + u0[bi] for bi in B_]
    yst = [_bdot(r_bar[bi], stb[bi], NT) for bi in B_]
    ub = [x.astype(bf16) for x in u]
    yu = fold({q: _bdot(prb[q], ub[q[0]]) for q in bh})
    upd = [_bdot(jnp.concatenate([u[bi], v[bi]], axis=0), jnp.concatenate([b_hat[bi], k_hat[bi]], axis=0), TN)
           for bi in B_]
    same = jnp.right_shift(_iota((RW_W, RW_W), 0), 6) == jnp.right_shift(_iota((RW_W, RW_W), 1), 6)
    for bi in B_:
        st_s[bi] = st[bi] * jnp.exp(cum_c[bi]) + jnp.where(same, upd[bi], 0.0)

    inv_n = 1.0 / RW_N
    y = [yst[bi] + y0[bi] + yu[bi] for bi in B_]
    mean = [_dotx(x, bd1, 2, 1) * inv_n for x in y]
    bsum = [_dotx(r[bi] * k2[bi] * rk_ref[...], bd1, 2, 1) for bi in B_]
    d = [y[bi] - mean[bi] for bi in B_]
    var = [_dotx(x * x, bd1, 2, 1) * inv_n for x in d]
    for bi in B_:
        yn = d[bi] * lax.rsqrt(var[bi] + RW_GN_EPS) * lnw_ref[...] + lnb_ref[...]
        o_ref[bi] = ((yn + bsum[bi] * v[bi]) * gate[bi]).astype(o_ref.dtype)

    @pl.when(t == nt - 1)
    def _():
        s_out_ref[...] = st_s[...]


def _rwkv(proj, row0, B, T, c, bb, parts, sh0, s0_bd, lw, out_dtype):
    nt = T // c
    rb0 = row0 // c
    pw = 256
    G = B // bb
    vec = lambda n: pl.BlockSpec((1, n), lambda g, t: (0, 0))
    mat = lambda r_, n: pl.BlockSpec((r_, n), lambda g, t: (0, 0))
    p_specs = [pl.BlockSpec((c, pw), lambda g, t, j=j, i=i: (rb0 + (g * bb + i) * nt + t, (CB_RW * LANES) // pw + j))
               for i in range(bb) for j in range(C_RWKV // pw)]
    hh = jnp.arange(RW_W) // RW_N
    bd1 = (hh[:, None] == hh[None, :]).astype(bf16)
    outs = pl.pallas_call(
        functools.partial(_rwkv_kernel, c=c, nt=nt, bb=bb, parts=parts),
        grid=(G, nt),
        in_specs=p_specs + [
            pl.BlockSpec((bb, 1, C_RWKV), lambda g, t: (g, 0, 0)),
            pl.BlockSpec((bb, RW_W, RW_W), lambda g, t: (g, 0, 0)),
            vec(C_RWKV), vec(RW_W), mat(LORA_W, RW_W), vec(RW_W), mat(LORA_A, RW_W), mat(LORA_G, RW_W),
            vec(RW_W), vec(RW_W), vec(RW_W), vec(RW_W), vec(RW_W), mat(RW_W, RW_W)],
        out_specs=[pl.BlockSpec((bb, c, RW_W), lambda g, t: (g, t, 0)),
                   pl.BlockSpec((bb, RW_W, RW_W), lambda g, t: (g, 0, 0)),
                   pl.BlockSpec((bb, 1, C_RWKV), lambda g, t: (g, 0, 0))],
        out_shape=[jax.ShapeDtypeStruct((B, T, RW_W), out_dtype),
                   jax.ShapeDtypeStruct((B, RW_W, RW_W), f32), jax.ShapeDtypeStruct((B, 1, C_RWKV), f32)],
        scratch_shapes=[pltpu.VMEM((bb, RW_W, RW_W), f32), pltpu.VMEM((bb, 1, C_RWKV), f32)],
        compiler_params=_cparams(("parallel", "arbitrary")), name="rwkv7",
    )(*([proj] * (bb * (C_RWKV // pw))), sh0.reshape(B, 1, C_RWKV), s0_bd,
      lw['mu'].reshape(1, -1), lw['w0'].reshape(1, -1), lw['w2'], lw['a0'].reshape(1, -1), lw['a2'], lw['g2'],
      lw['kk'].reshape(1, -1), lw['ka'].reshape(1, -1), lw['rk'].reshape(1, -1),
      lw['lnx_w'].reshape(1, -1), lw['lnx_b'].reshape(1, -1), bd1)
    return outs[0].reshape(B * T, RW_W), outs[1], outs[2]


def _to_blockdiag(s):
    B = s.shape[0]
    eye = jnp.eye(RW_H, dtype=s.dtype)
    return jnp.einsum('bhvk,hg->bhvgk', s, eye).reshape(B, RW_W, RW_W)


def _from_blockdiag(s):
    B = s.shape[0]
    s5 = s.reshape(B, RW_H, RW_N, RW_H, RW_N)
    idx = jnp.arange(RW_H)
    return s5[:, idx, :, idx, :].transpose(1, 0, 2, 3)


def _fox_prep_kernel(q_ref, k_ref, v_ref, qg_ref, kg_ref, qn_ref, kn_ref, kb_ref, vn_ref, vb_ref):
    def nrm(x, g):
        return x * lax.rsqrt(jnp.mean(x * x, axis=-1, keepdims=True) + EPS) * g
    qn_ref[...] = nrm(q_ref[...], qg_ref[...]).astype(qn_ref.dtype)
    kn = nrm(k_ref[...], kg_ref[...])
    kn_ref[...] = kn
    kb_ref[...] = kn.astype(kb_ref.dtype)
    v = v_ref[...]
    vn_ref[...] = v
    vb_ref[...] = v.astype(vb_ref.dtype)


def _fox_prep(proj, row0, rows, tt, qg, kg, lowp):
    rb0 = row0 // tt
    col = lambda cb: pl.BlockSpec((tt, FX_DH), lambda r, h, cb=cb: (rb0 + r, cb + h))
    out = pl.BlockSpec((tt, FX_DH), lambda r, h: (r, h))
    gsp = pl.BlockSpec((1, FX_DH), lambda r, h: (0, 0))
    sds = lambda dt: jax.ShapeDtypeStruct((rows, FX_W), dt)
    return pl.pallas_call(
        _fox_prep_kernel, grid=(rows // tt, FX_H),
        in_specs=[col(CB_FQ), col(CB_FK), col(CB_FV), gsp, gsp],
        out_specs=[out] * 5, out_shape=[sds(lowp), sds(f32), sds(lowp), sds(f32), sds(lowp)],
        compiler_params=_cparams(("parallel", "parallel")), name="fox_prep",
    )(proj, proj, proj, qg.reshape(1, FX_DH), kg.reshape(1, FX_DH))


def _fox_lf_kernel(ff_ref, b_ref, lf_ref, cum_ref, carry_s):
    tt = ff_ref.shape[0]

    @pl.when(pl.program_id(1) == 0)
    def _():
        carry_s[...] = jnp.zeros_like(carry_s)

    lf = jax.nn.log_sigmoid(ff_ref[...] + b_ref[...])
    tril = jnp.where(_iota((tt, tt), 0) >= _iota((tt, tt), 1), 1.0, 0.0).astype(bf16)
    cum = _dotx(tril, lf, 1, 3) + carry_s[...]
    carry_s[...] = cum[tt - 1:tt]
    lf_ref[...] = lf[:, :FX_H]
    cum_ref[...] = cum[:, :FX_H]


def _fox_lf(ffl, row0, B, T, tt, fbias):
    nt = T // tt
    rb0 = row0 // tt
    bias = jnp.zeros((1, LANES), f32).at[0, :FX_H].set(fbias)
    out = pl.BlockSpec((tt, FX_H), lambda b, t: (b * nt + t, 0))
    return pl.pallas_call(
        _fox_lf_kernel, grid=(B, nt),
        in_specs=[pl.BlockSpec((tt, LANES), lambda b, t: (rb0 + b * nt + t, 0)),
                  pl.BlockSpec((1, LANES), lambda b, t: (0, 0))],
        out_specs=[out, out], out_shape=[jax.ShapeDtypeStruct((B * T, FX_H), f32)] * 2,
        scratch_shapes=[pltpu.VMEM((1, LANES), f32)],
        compiler_params=_cparams(("parallel", "arbitrary")), name="fox_logf",
    )(ffl, bias)


def _fox_attn_kernel(qt_ref, kt_ref, q_ref, k_ref, v_ref, cq_ref, ck_ref, o_ref, m_s, l_s, acc_s, *, tq, tk):
    step = pl.program_id(2)
    qi = qt_ref[step]
    ki = kt_ref[step]

    @pl.when(ki == 0)
    def _():
        m_s[...] = jnp.full_like(m_s, -jnp.inf)
        l_s[...] = jnp.zeros_like(l_s)
        acc_s[...] = jnp.zeros_like(acc_s)

    nh = q_ref.shape[1] // FX_DH

    def block(masked):
        hs = [slice(i * FX_DH, (i + 1) * FX_DH) for i in range(nh)]
        ss = [lax.dot_general(q_ref[:, hs[i]], k_ref[:, hs[i]], NT, preferred_element_type=f32) for i in range(nh)]
        ps = []
        for i in range(nh):
            s = ss[i] * (FX_DH ** -0.5) + (cq_ref[i] - ck_ref[i])
            if masked:
                qpos = qi * tq + _iota((tq, tk), 0)
                kpos = ki * tk + _iota((tq, tk), 1)
                s = jnp.where(kpos <= qpos, s, -jnp.inf)
            m_old = m_s[i]
            m_new = jnp.maximum(m_old, jnp.max(s, axis=-1, keepdims=True))
            alpha = jnp.exp(m_old - m_new)
            p = jnp.exp(s - m_new)
            l_s[i] = alpha * l_s[i] + jnp.sum(p, axis=-1, keepdims=True)
            acc_s[:, hs[i]] = alpha * acc_s[:, hs[i]]
            m_s[i] = m_new
            ps.append(p.astype(bf16))
        for i in range(nh):
            acc_s[:, hs[i]] = acc_s[:, hs[i]] + jnp.dot(ps[i], v_ref[:, hs[i]], preferred_element_type=f32)

    on_diagonal = ki == qi
    pl.when(jnp.logical_not(on_diagonal))(functools.partial(block, False))
    pl.when(on_diagonal)(functools.partial(block, True))

    @pl.when(on_diagonal)
    def _():
        for i in range(nh):
            hs = slice(i * FX_DH, (i + 1) * FX_DH)
            o_ref[:, hs] = (acc_s[:, hs] / l_s[i]).astype(o_ref.dtype)


def _fox_attn(qn, kb, vb, cum, B, T, tq, nh=2):
    tk = tq
    nq = T // tq
    cq = cum.reshape(B, T, FX_H).transpose(0, 2, 1).reshape(B, FX_H, T, 1)
    ck = cq.reshape(B, FX_H, 1, T)
    pairs = [(q, k) for q in range(nq) for k in range(q + 1)]
    qt = jnp.asarray([p[0] for p in pairs], i32)
    kt = jnp.asarray([p[1] for p in pairs], i32)
    w = nh * FX_DH
    kv = pl.BlockSpec((tk, w), lambda b, h, s, qt_, kt_: (b * nq + kt_[s], h))
    grid_spec = pltpu.PrefetchScalarGridSpec(
        num_scalar_prefetch=2, grid=(B, FX_H // nh, len(pairs)),
        in_specs=[pl.BlockSpec((tq, w), lambda b, h, s, qt_, kt_: (b * nq + qt_[s], h)), kv, kv,
                  pl.BlockSpec((None, nh, tq, 1), lambda b, h, s, qt_, kt_: (b, h, qt_[s], 0)),
                  pl.BlockSpec((None, nh, 1, tk), lambda b, h, s, qt_, kt_: (b, h, 0, kt_[s]))],
        out_specs=pl.BlockSpec((tq, w), lambda b, h, s, qt_, kt_: (b * nq + qt_[s], h)),
        scratch_shapes=[pltpu.VMEM((nh, tq, 1), f32), pltpu.VMEM((nh, tq, 1), f32), pltpu.VMEM((tq, w), f32)])
    return pl.pallas_call(
        functools.partial(_fox_attn_kernel, tq=tq, tk=tk), grid_spec=grid_spec,
        out_shape=jax.ShapeDtypeStruct((B * T, FX_W), bf16),
        compiler_params=_cparams(("parallel", "parallel", "arbitrary")), name="fox_attn",
    )(qt, kt, qn, kb, vb, cq, ck)


def _rep_rows(x, td):
    return jnp.concatenate([jnp.broadcast_to(x[h:h + 1], (td, x.shape[1])) for h in range(x.shape[0])], axis=0)


def _fox_dec_kernel(pt_ref, q_ref, kn_ref, vn_ref, cqc_ref, cqr_ref, *refs, td, n_steps, pps, nb):
    per = 3 * pps
    kc_refs = [refs[b * per:b * per + pps] for b in range(nb)]
    vc_refs = [refs[b * per + pps:b * per + 2 * pps] for b in range(nb)]
    lft_refs = [refs[b * per + 2 * pps:b * per + 3 * pps] for b in range(nb)]
    o_ref, m_s, l_s, acc_s, suf_s = refs[nb * per:]
    j = pl.program_id(1)
    nr = FX_H * td
    scale = FX_DH ** -0.5
    B_ = range(nb)
    H_ = range(FX_H)
    hcols = [slice(h * FX_DH, (h + 1) * FX_DH) for h in H_]
    hrows = [slice(h * td, (h + 1) * td) for h in H_]
    brows = [slice(b * td, (b + 1) * td) for b in B_]

    def heads_q(b):
        q = q_ref[brows[b], :]
        return [q[:, hcols[h]].astype(bf16) for h in H_]

    def softmax_step(b, s):
        m_old = m_s[b]
        m_new = jnp.maximum(m_old, jnp.max(s, axis=-1, keepdims=True))
        alpha = jnp.exp(m_old - m_new)
        p = jnp.exp(s - m_new)
        l_s[b] = alpha * l_s[b] + jnp.sum(p, axis=-1, keepdims=True)
        m_s[b] = m_new
        return alpha, p

    def apply_values(b, alpha, p, v_of):
        pv = []
        for h in H_:
            acc = None
            for i in range(p.shape[1] // PAGE):
                t = jnp.dot(p[hrows[h], i * PAGE:(i + 1) * PAGE].astype(bf16), v_of(i, h), preferred_element_type=f32)
                acc = t if acc is None else acc + t
            pv.append(acc)
        acc_s[b] = alpha * acc_s[b] + jnp.concatenate(pv, axis=0)

    @pl.when(j == 0)
    def _():
        pad = jnp.zeros((PAGE - td, FX_W), f32)
        m_s[...] = jnp.full_like(m_s, -jnp.inf)
        l_s[...] = jnp.zeros_like(l_s)
        acc_s[...] = jnp.zeros_like(acc_s)
        suf_s[...] = jnp.zeros_like(suf_s)
        kpos = _iota((nr, PAGE), 1)
        qpos = jnp.bitwise_and(_iota((nr, PAGE), 0), td - 1)
        ss, vps = [], []
        for b in B_:
            qh = heads_q(b)
            kp = jnp.concatenate([kn_ref[brows[b], :], pad], axis=0).astype(bf16)
            vps.append(jnp.concatenate([vn_ref[brows[b], :], pad], axis=0).astype(bf16))
            s = jnp.concatenate([lax.dot_general(qh[h], kp[:, hcols[h]], NT, preferred_element_type=f32)
                                 for h in H_], axis=0) * scale
            s = s + (cqc_ref[b] - _rep_rows(cqr_ref[b], td))
            ss.append(jnp.where(kpos <= qpos, s, -jnp.inf))
        ap = [softmax_step(b, ss[b]) for b in B_]
        for b in B_:
            apply_values(b, ap[b][0], ap[b][1], lambda i, h, b=b: vps[b][:, hcols[h]])

    @pl.when(j > 0)
    def _():
        later = jnp.where(_iota((PAGE, PAGE), 0) > _iota((PAGE, PAGE), 1), 1.0, 0.0).astype(bf16)
        ones = jnp.ones((PAGE, PAGE), bf16)
        ss = []
        for b in B_:
            qh = heads_q(b)
            cq = cqc_ref[b]
            after = suf_s[b]
            parts = []
            for i in range(pps):
                lf = lft_refs[b][i][...]
                suf = _dotx(lf, later, 3, 1, TN) + after
                after = after + _dotx(lf, ones, 3, 1, TN)
                s = jnp.concatenate(
                    [lax.dot_general(qh[h], kc_refs[b][i][pl.ds(h, PAGE, stride=FX_H), :].astype(bf16), NT,
                                     preferred_element_type=f32) for h in H_], axis=0) * scale
                parts.append(s + (cq + _rep_rows(suf, td)))
            suf_s[b] = after
            ss.append(jnp.concatenate(parts, axis=1))
        ap = [softmax_step(b, ss[b]) for b in B_]
        for b in B_:
            apply_values(b, ap[b][0], ap[b][1],
                         lambda i, h, b=b: vc_refs[b][i][pl.ds(h, PAGE, stride=FX_H), :].astype(bf16))

    @pl.when(j == n_steps - 1)
    def _():
        for b in B_:
            a = acc_s[b] / l_s[b]
            for h in H_:
                o_ref[brows[b], hcols[h]] = a[hrows[h]].astype(o_ref.dtype)


def _fox_decode(qn, kn, vn, cnew, kcache, vcache, lft, page_table, layer, td):
    Bd, n_pages = page_table.shape
    pps = 4 if n_pages % 4 == 0 else (2 if n_pages % 2 == 0 else 1)
    nb = 2 if Bd % 2 == 0 else 1
    n_steps = n_pages // pps + 1
    nr = FX_H * td
    c3 = cnew.reshape(Bd, td, FX_H).transpose(0, 2, 1)
    cqc = c3.reshape(Bd, nr, 1)
    cqr = jnp.zeros((Bd, FX_H, PAGE), f32).at[:, :, :td].set(c3)

    def page(b, i, nd):
        return lambda g, j, pt: (layer, pt[g * nb + b, n_pages - 1 - i - (jnp.maximum(j, 1) - 1) * pps]) + (0,) * nd

    tok = pl.BlockSpec((nb * td, FX_W), lambda g, j, pt: (g, 0))
    pages = []
    for b in range(nb):
        kvs = [pl.BlockSpec((None, None, PAGE * FX_H, FX_DH), page(b, i, 2)) for i in range(pps)]
        pages += kvs + kvs + [pl.BlockSpec((None, None, PAGE, FX_H), page(b, i, 2)) for i in range(pps)]
    grid_spec = pltpu.PrefetchScalarGridSpec(
        num_scalar_prefetch=1, grid=(Bd // nb, n_steps),
        in_specs=[tok, tok, tok,
                  pl.BlockSpec((nb, nr, 1), lambda g, j, pt: (g, 0, 0)),
                  pl.BlockSpec((nb, FX_H, PAGE), lambda g, j, pt: (g, 0, 0))] + pages,
        out_specs=pl.BlockSpec((nb * td, FX_W), lambda g, j, pt: (g, 0)),
        scratch_shapes=[pltpu.VMEM((nb, nr, 1), f32), pltpu.VMEM((nb, nr, 1), f32), pltpu.VMEM((nb, nr, FX_DH), f32),
                        pltpu.VMEM((nb, FX_H, PAGE), f32)])
    return pl.pallas_call(
        functools.partial(_fox_dec_kernel, td=td, n_steps=n_steps, pps=pps, nb=nb), grid_spec=grid_spec,
        out_shape=jax.ShapeDtypeStruct((Bd * td, FX_W), f32),
        compiler_params=_cparams(("parallel", "arbitrary")), name="fox_decode",
    )(page_table, qn, kn, vn, cqc, cqr, *(([kcache] * pps + [vcache] * pps + [lft] * pps) * nb))


def _moe_plan_kernel(r_ref, pos_ref, te_ref, cs_s, *, rb, tm_e):
    M = r_ref.shape[0]
    nb = M // rb
    lane = _iota((rb, LANES), 1)
    strict = jnp.where(_iota((rb, rb), 0) > _iota((rb, rb), 1), 1.0, 0.0).astype(bf16)

    def onehots(blk):
        r = r_ref[blk * rb:(blk + 1) * rb, :]
        i1 = r[:, 0:1].astype(i32)
        i2 = r[:, 1:2].astype(i32)
        return lane == i1, lane == i2

    carry = jnp.zeros((1, LANES), f32)
    for blk in range(nb):
        o1, o2 = onehots(blk)
        oh = jnp.where(o1, 1.0, jnp.where(o2, 1.0, 0.0))
        cs_s[blk * rb:(blk + 1) * rb, :] = _dotx(strict, oh.astype(bf16), 1, 1) + carry
        carry = carry + jnp.sum(oh, axis=0, keepdims=True)
    padded = jnp.floor((carry + (tm_e - 1)) * (1.0 / tm_e)) * tm_e
    before = jnp.where(_iota((LANES, LANES), 0) < _iota((LANES, LANES), 1), 1.0, 0.0).astype(bf16)
    off = _dotx(jnp.broadcast_to(padded, (8, LANES)), before, 3, 1)[0:1]
    end = off + padded
    for blk in range(nb):
        o1, o2 = onehots(blk)
        base = off + cs_s[blk * rb:(blk + 1) * rb, :]
        p1 = jnp.sum(jnp.where(o1, base, 0.0), axis=-1, keepdims=True)
        p2 = jnp.sum(jnp.where(o2, base, 0.0), axis=-1, keepdims=True)
        pos_ref[blk * rb:(blk + 1) * rb, :] = jnp.where(lane == 0, p1, jnp.where(lane == 1, p2, 0.0)).astype(i32)
    l1 = _iota((1, LANES), 1)
    start = (l1 * tm_e).astype(f32)
    te = jnp.zeros((1, LANES), f32)
    for e in range(N_EXPERTS):
        end_e = jnp.sum(jnp.where(l1 == e, end, 0.0), axis=-1, keepdims=True)
        te = te + jnp.where(end_e <= start, 1.0, 0.0)
    total = jnp.sum(jnp.where(l1 == N_EXPERTS - 1, end, 0.0), axis=-1, keepdims=True)
    te = jnp.minimum(te, N_EXPERTS - 1.0)
    te_ref[...] = jnp.where(l1 == LANES - 1, total * (1.0 / tm_e), te).astype(i32)


def _moe_plan(route, tm_e, rb):
    M = route.shape[0]
    return pl.pallas_call(
        functools.partial(_moe_plan_kernel, rb=rb, tm_e=tm_e),
        out_shape=[jax.ShapeDtypeStruct((M, LANES), i32), jax.ShapeDtypeStruct((1, LANES), i32)],
        scratch_shapes=[pltpu.VMEM((M, LANES), f32)],
        compiler_params=pltpu.CompilerParams(vmem_limit_bytes=VMEM_LIMIT_BYTES), name="moe_plan")(route)


def _moe_gather_kernel(pos_ref, te_ref, u_ref, xs_ref, src_s, buf, sem, *, tm_e, n_pairs):
    t = pl.program_id(0)
    rows = src_s.shape[0]

    @pl.when(t == 0)
    def _():
        def clear(i, c):
            src_s[i] = 0
            return c

        def put(i, c):
            src_s[pos_ref[i]] = jnp.right_shift(i, 1)
            return c

        lax.fori_loop(0, rows, clear, 0, unroll=8)
        lax.fori_loop(0, n_pairs, put, 0, unroll=8)

    ns = u_ref.shape[1]

    def copy(r):
        return pltpu.make_async_copy(u_ref.at[src_s[t * tm_e + r]], buf.at[pl.ds(pl.multiple_of(r * ns, ns), ns)], sem)

    @pl.when(t < te_ref[LANES - 1])
    def _():
        def start(r, c):
            copy(r).start()
            return c

        def wait(r, c):
            copy(r).wait()
            return c

        lax.fori_loop(0, tm_e, start, 0, unroll=8)
        lax.fori_loop(0, tm_e, wait, 0, unroll=8)
        for s in range(ns):
            xs_ref[:, s * LANES:(s + 1) * LANES] = buf[pl.ds(s, tm_e, stride=ns), :].astype(xs_ref.dtype)

    @pl.when(t >= te_ref[LANES - 1])
    def _():
        xs_ref[...] = jnp.zeros_like(xs_ref)


def _moe_gather(u3, pos_flat, te, rows, tm_e):
    M, S, _ = u3.shape
    grid_spec = pltpu.PrefetchScalarGridSpec(
        num_scalar_prefetch=2, grid=(rows // tm_e,),
        in_specs=[pl.BlockSpec(memory_space=pl.ANY)],
        out_specs=pl.BlockSpec((tm_e, S * LANES), lambda t, p, te_: (t, 0)),
        scratch_shapes=[pltpu.SMEM((rows,), i32), pltpu.VMEM((tm_e * S, LANES), f32), pltpu.SemaphoreType.DMA(())])
    return pl.pallas_call(
        functools.partial(_moe_gather_kernel, tm_e=tm_e, n_pairs=2 * M), grid_spec=grid_spec,
        out_shape=jax.ShapeDtypeStruct((rows, S * LANES), bf16),
        compiler_params=_cparams(("arbitrary",)), name="moe_gather")(pos_flat, te, u3)


def _moe_mm_kernel(te_ref, *refs, glu):
    it = iter(refs)
    a_ref = next(it)
    w_ref = next(it)
    w3_ref = next(it) if glu else None
    o_ref = next(it)
    wb_ref = next(it)
    wb3_ref = next(it) if glu else None
    t = pl.program_id(1)
    e = te_ref[t]
    e_prev = te_ref[jnp.maximum(t - 1, 0)]

    @pl.when((t == 0) | (e != e_prev))
    def _():
        wb_ref[...] = w_ref[...].astype(bf16)
        if glu:
            wb3_ref[...] = w3_ref[...].astype(bf16)

    @pl.when(t < te_ref[LANES - 1])
    def _():
        a = a_ref[...]
        y = jnp.dot(a, wb_ref[...], preferred_element_type=f32)
        if glu:
            y = jax.nn.silu(y) * jnp.dot(a, wb3_ref[...], preferred_element_type=f32)
        if len(o_ref.shape) == 3:
            for s in range(o_ref.shape[1]):
                o_ref[:, s, :] = y[:, s * LANES:(s + 1) * LANES].astype(o_ref.dtype)
        else:
            o_ref[...] = y.astype(o_ref.dtype)

    @pl.when(t >= te_ref[LANES - 1])
    def _():
        o_ref[...] = jnp.zeros_like(o_ref)


def _moe_matmul(a, te, w, fi, tm_e, tn, w3=None, out_dtype=f32, slabs=False, name="moe_matmul"):
    R, K = a.shape
    N = w.shape[-1]
    glu = w3 is not None
    w_spec = pl.BlockSpec((None, None, K, tn), lambda n, t, te_: (fi, te_[t], 0, n))
    in_specs = [pl.BlockSpec((tm_e, K), lambda n, t, te_: (t, 0)), w_spec] + ([w_spec] if glu else [])
    args = [a, w] + ([w3] if glu else [])
    scratch = [pltpu.VMEM((K, tn), bf16)] * (2 if glu else 1)
    if slabs:
        out_shape = jax.ShapeDtypeStruct((R, N // tn, tn // LANES, LANES), out_dtype)
        out_spec = pl.BlockSpec((tm_e, None, tn // LANES, LANES), lambda n, t, te_: (t, n, 0, 0))
    else:
        out_shape = jax.ShapeDtypeStruct((R, N), out_dtype)
        out_spec = pl.BlockSpec((tm_e, tn), lambda n, t, te_: (t, n))
    grid_spec = pltpu.PrefetchScalarGridSpec(
        num_scalar_prefetch=1, grid=(N // tn, R // tm_e), in_specs=in_specs,
        out_specs=out_spec, scratch_shapes=scratch)
    return pl.pallas_call(
        functools.partial(_moe_mm_kernel, glu=glu), grid_spec=grid_spec, out_shape=out_shape,
        compiler_params=_cparams(("arbitrary", "arbitrary")), name=name)(te, *args)


def _moe_combine_kernel(pos_ref, h_ref, r_ref, y_ref, o_ref, buf0, buf1, sem, *, tb):
    i0 = pl.program_id(0) * tb
    bufs = (buf0, buf1)
    ns = y_ref.shape[1]

    def copy(i, j):
        return pltpu.make_async_copy(y_ref.at[pos_ref[2 * (i0 + i) + j]],
                                     bufs[j].at[pl.ds(pl.multiple_of(i * ns, ns), ns)], sem)

    def start(i, c):
        copy(i, 0).start()
        copy(i, 1).start()
        return c

    def wait(i, c):
        copy(i, 0).wait()
        copy(i, 1).wait()
        return c

    lax.fori_loop(0, tb, start, 0, unroll=8)
    lax.fori_loop(0, tb, wait, 0, unroll=8)
    r = r_ref[...]
    g1 = r[:, 2:3]
    g2 = r[:, 3:4]
    for s in range(ns):
        c0 = s * LANES
        y0 = buf0[pl.ds(s, tb, stride=ns), :]
        y1 = buf1[pl.ds(s, tb, stride=ns), :]
        o_ref[:, c0:c0 + LANES] = h_ref[:, c0:c0 + LANES] + (g1 * y0 + g2 * y1)


def _moe_combine(h, route, y3, pos_flat, tb):
    M, D = h.shape
    grid_spec = pltpu.PrefetchScalarGridSpec(
        num_scalar_prefetch=1, grid=(M // tb,),
        in_specs=[pl.BlockSpec((tb, D), lambda m, p: (m, 0)), pl.BlockSpec((tb, LANES), lambda m, p: (m, 0)),
                  pl.BlockSpec(memory_space=pl.ANY)],
        out_specs=pl.BlockSpec((tb, D), lambda m, p: (m, 0)),
        scratch_shapes=[pltpu.VMEM((tb * y3.shape[1], LANES), f32), pltpu.VMEM((tb * y3.shape[1], LANES), f32),
                        pltpu.SemaphoreType.DMA(())])
    return pl.pallas_call(
        functools.partial(_moe_combine_kernel, tb=tb), grid_spec=grid_spec,
        out_shape=jax.ShapeDtypeStruct((M, D), f32),
        compiler_params=_cparams(("arbitrary",)), name="moe_combine")(pos_flat, h, route, y3)


def _moe(h, u3, route, w1, w3, w2, fi, tm, tm_e=256):
    M = h.shape[0]
    rows = (pl.cdiv(2 * M, tm_e) + N_EXPERTS) * tm_e
    pos, te = _moe_plan(route, tm_e, tm)
    pos_flat = pos[:, :2].reshape(-1)
    te = te.reshape(-1)
    xs = _moe_gather(u3, pos_flat, te, rows, tm_e)
    act = _moe_matmul(xs, te, w1, fi, tm_e, 1024, w3=w3, out_dtype=bf16, name="moe_up")
    y = _moe_matmul(act, te, w2, fi, tm_e, 512, slabs=True, name="moe_down")
    y3 = y.reshape(rows, h.shape[1] // LANES, LANES)
    return _moe_combine(h, route, y3, pos_flat, tm // 2)


def kernel(x_prompt, x_sample, cache_fox_k, cache_fox_v, cache_fox_logf, page_table, state_hgrn, state_rwkv, state_rwkv_shift, norm_mix, w_in, w_out, hgrn_lb, hgrn_norm, rwkv_mu, rwkv_w0, rwkv_w2, rwkv_a0, rwkv_a2, rwkv_g2, rwkv_kk, rwkv_ka, rwkv_rk, rwkv_lnx_w, rwkv_lnx_b, fox_qnorm, fox_knorm, fox_fbias, norm_ffn, ffn_w1, ffn_w3, ffn_w2, moe_router, moe_w1, moe_w3, moe_w2):
    B, T, D = x_prompt.shape
    Bd, Td, _ = x_sample.shape
    depth = w_in.shape[0]
    n_pool = cache_fox_k.shape[1]
    Mp, Ms = B * T, Bd * Td
    M = Mp + Ms
    tm = 688 if M % 688 == 0 else M
    d_ff = ffn_w1.shape[-1]

    h = jnp.concatenate([x_prompt.reshape(Mp, D), x_sample.reshape(Ms, D)], axis=0)
    lb_all = jnp.cumsum(jax.nn.softmax(hgrn_lb.astype(f32), axis=0), axis=0)
    lb_all = lb_all - lb_all[:1]
    kcache = cache_fox_k.reshape(depth, n_pool, PAGE * FX_H, FX_DH)
    vcache = cache_fox_v.reshape(depth, n_pool, PAGE * FX_H, FX_DH)
    lft = cache_fox_logf
    zeros_hg = jnp.zeros((B, HG_H, HG_D, HG_D), f32)
    zeros_rw = jnp.zeros((B, RW_W, RW_W), f32)
    zeros_sh = jnp.zeros((B, C_RWKV), f32)
    c_p = 64 if T % 64 == 0 else T
    tb_p = 256 if T % 256 == 0 else T
    tt_p = 512 if T % 512 == 0 else T

    outs = [[] for _ in range(12)]
    for l in range(depth):
        fi = l // 2
        rw = dict(mu=rwkv_mu[l], w0=rwkv_w0[l], w2=rwkv_w2[l], a0=rwkv_a0[l], a2=rwkv_a2[l], g2=rwkv_g2[l],
                  kk=rwkv_kk[l], ka=rwkv_ka[l], rk=rwkv_rk[l], lnx_w=rwkv_lnx_w[l], lnx_b=rwkv_lnx_b[l])
        u, ffl = _rmsnorm(h, norm_mix[l], w_small=w_in[l][:, N_PROJ:], tm=tm)
        proj = _matmul(u, w_in, l, N_PROJ, 768, tm, name="mm_in")

        o_hg, s_hg = _hgrn(proj, 0, B, T, lb_all[l], hgrn_norm[l], zeros_hg, 16 if tb_p % 16 == 0 else tb_p, tb_p, bf16)
        o_rw, s_rw, sh = _rwkv(proj, 0, B, T, c_p, B, 1, zeros_sh, zeros_rw, rw, bf16)
        qn, kn, kb, vn, vb = _fox_prep(proj, 0, Mp, min(1024, Mp), fox_qnorm[l], fox_knorm[l], bf16)
        lf, cum = _fox_lf(ffl, 0, B, T, tt_p, fox_fbias[l])
        o_fx = _fox_attn(qn, kb, vb, cum, B, T, tt_p)
        for i, x in enumerate((kn.reshape(B, T, FX_H, FX_DH), vn.reshape(B, T, FX_H, FX_DH),
                               lf.reshape(B, T, FX_H), s_hg, _from_blockdiag(s_rw), sh.reshape(B, C_RWKV))):
            outs[i].append(x)

        o_hg_s, s_hg_s = _hgrn(proj, Mp, Bd, Td, lb_all[l], hgrn_norm[l], state_hgrn[l], Td, Td, f32)
        o_rw_s, s_rw_s, sh_s = _rwkv(proj, Mp, Bd, Td, Td, 1, 2, state_rwkv_shift[l], _to_blockdiag(state_rwkv[l]),
                                     rw, f32)
        qn_s, kn_s, _, vn_s, _ = _fox_prep(proj, Mp, Ms, Ms, fox_qnorm[l], fox_knorm[l], f32)
        lf_s, cnew = _fox_lf(ffl, Mp, Bd, Td, Td, fox_fbias[l])
        o_fx_s = _fox_decode(qn_s, kn_s, vn_s, cnew, kcache, vcache, lft, page_table, l, Td)
        for i, x in enumerate((kn_s.reshape(Bd, Td, FX_H, FX_DH), vn_s.reshape(Bd, Td, FX_H, FX_DH),
                               lf_s.reshape(Bd, Td, FX_H), s_hg_s, _from_blockdiag(s_rw_s), sh_s.reshape(Bd, C_RWKV))):
            outs[6 + i].append(x)

        h = _mm_out(h, [o_hg, o_rw, o_fx], 0, min(512, Mp), w_out, l)
        h = _mm_out(h, [o_hg_s, o_rw_s, o_fx_s], Mp, Ms, w_out, l)
        if l % 2 == 0:
            u2 = _rmsnorm(h, norm_ffn[l], tm=tm)
            act = _matmul(u2, ffn_w1, fi, d_ff, 512, tm, w3=ffn_w3, out_dtype=bf16, name="ffn_up")
            h = _matmul(act, ffn_w2, fi, D, 512, tm, res=h, name="ffn_down")
        else:
            u3, route = _rmsnorm(h, norm_ffn[l], w_small=moe_router[fi], route=True, tm=tm, slabs=True)
            h = _moe(h, u3, route, moe_w1, moe_w3, moe_w2, fi, tm)

    st = [jnp.stack(o) for o in outs]
    return (h[:Mp].reshape(B, T, D), h[Mp:].reshape(Bd, Td, D), *st)
```

```python
import functools
import math

import jax
import jax.numpy as jnp
from jax import lax
from jax.experimental import pallas as pl
from jax.experimental.pallas import tpu as pltpu

f32 = jnp.float32
bf16 = jnp.bfloat16
i32 = jnp.int32

D_MODEL = 2048
DEPTH = 4
PAGE = 128
HG_H, HG_D = 4, 128
HG_W = HG_H * HG_D
RW_H, RW_N = 8, 64
RW_W = RW_H * RW_N
LORA_W, LORA_A, LORA_G = 64, 64, 128
C_RWKV = 3 * RW_W + LORA_W + LORA_A + LORA_G
FX_H, FX_DH = 8, 128
FX_W = FX_H * FX_DH
N_EXPERTS = 8
EPS = 1e-6
RW_GN_EPS = 64e-5
CB_HG = 0
CB_RW = (4 * HG_W) // 128
CB_FQ = CB_RW + C_RWKV // 128
CB_FK = CB_FQ + FX_H
CB_FV = CB_FK + FX_H
N_PROJ = (CB_FV + FX_H) * 128

LANES = 128
VMEM_LIMIT_BYTES = 56 * 1024 * 1024

NT = (((1,), (1,)), ((), ()))
TN = (((0,), (0,)), ((), ()))


def _cparams(sem):
    return pltpu.CompilerParams(dimension_semantics=sem, vmem_limit_bytes=VMEM_LIMIT_BYTES)


def _split(x, n):
    parts, r = [], x
    for i in range(n):
        p = r.astype(bf16)
        parts.append(p)
        if i + 1 < n:
            r = r - p.astype(f32)
    return parts


def _dotx(a, b, na, nb, dims=None):
    ap = _split(a, na) if a.dtype != bf16 else [a]
    bp = _split(b, nb) if b.dtype != bf16 else [b]
    n = max(len(ap), len(bp))
    acc = None
    for i, x in enumerate(ap):
        for j, y in enumerate(bp):
            if i + j < n:
                t = (jnp.dot(x, y, preferred_element_type=f32) if dims is None
                     else lax.dot_general(x, y, dims, preferred_element_type=f32))
                acc = t if acc is None else acc + t
    return acc


def _bdot(a, b, dims=None):
    return _dotx(a.astype(bf16), b.astype(bf16), 1, 1, dims)


def _iota(shape, axis):
    return lax.broadcasted_iota(i32, shape, axis)


def _route(lg):
    lane = _iota(lg.shape, 1)
    l1 = jnp.where(lane < N_EXPERTS, lg, -jnp.inf)
    m1 = jnp.max(l1, axis=-1, keepdims=True)
    i1 = jnp.min(jnp.where(l1 == m1, lane, LANES), axis=-1, keepdims=True)
    l2 = jnp.where(lane == i1, -jnp.inf, l1)
    m2 = jnp.max(l2, axis=-1, keepdims=True)
    i2 = jnp.min(jnp.where(l2 == m2, lane, LANES), axis=-1, keepdims=True)
    e = jnp.exp(m2 - m1)
    den = 1.0 + e
    g1 = 1.0 / den
    g2 = e / den
    return jnp.where(lane == 0, i1.astype(f32),
                     jnp.where(lane == 1, i2.astype(f32),
                               jnp.where(lane == 2, g1, jnp.where(lane == 3, g2, 0.0))))


def _norm_kernel(*refs, has_small, route):
    if has_small:
        x_ref, g_ref, ws_ref, u_ref, s_ref = refs
    else:
        x_ref, g_ref, u_ref = refs
    x = x_ref[...]
    ms = jnp.mean(x * x, axis=-1, keepdims=True)
    u = x * lax.rsqrt(ms + EPS) * g_ref[...]
    if u_ref.dtype == f32:
        ns = u.shape[1] // LANES
        for s in range(ns):
            u_ref[pl.ds(s, u.shape[0], stride=ns), :] = u[:, s * LANES:(s + 1) * LANES]
    else:
        u_ref[...] = u.astype(bf16)
    if has_small:
        s = _dotx(u, ws_ref[...], 2, 2)
        s_ref[...] = _route(s) if route else s


def _rmsnorm(x, g, w_small=None, route=False, tm=688, slabs=False):
    M, D = x.shape
    has_small = w_small is not None
    in_specs = [pl.BlockSpec((tm, D), lambda m: (m, 0)), pl.BlockSpec((1, D), lambda m: (0, 0))]
    args = [x, g.reshape(1, D)]
    if slabs:
        out_shape = [jax.ShapeDtypeStruct((M * (D // LANES), LANES), f32)]
        out_specs = [pl.BlockSpec((tm * (D // LANES), LANES), lambda m: (m, 0))]
    else:
        out_shape = [jax.ShapeDtypeStruct((M, D), bf16)]
        out_specs = [pl.BlockSpec((tm, D), lambda m: (m, 0))]
    if has_small:
        ws = jnp.zeros((D, LANES), f32).at[:, :w_small.shape[1]].set(w_small)
        in_specs.append(pl.BlockSpec((D, LANES), lambda m: (0, 0)))
        args.append(ws)
        out_shape.append(jax.ShapeDtypeStruct((M, LANES), f32))
        out_specs.append(pl.BlockSpec((tm, LANES), lambda m: (m, 0)))
    outs = pl.pallas_call(
        functools.partial(_norm_kernel, has_small=has_small, route=route),
        grid=(M // tm,), in_specs=in_specs, out_specs=out_specs, out_shape=out_shape,
        compiler_params=_cparams(("parallel",)), name="rmsnorm")(*args)
    outs = list(outs)
    if slabs:
        outs[0] = outs[0].reshape(M, D // LANES, LANES)
    return outs if has_small else outs[0]


def _mm_kernel(*refs, glu, has_res):
    it = iter(refs)
    a_ref = next(it)
    w_ref = next(it)
    w3_ref = next(it) if glu else None
    r_ref = next(it) if has_res else None
    o_ref = next(it)
    wb_ref = next(it)
    wb3_ref = next(it) if glu else None

    @pl.when(pl.program_id(1) == 0)
    def _():
        wb_ref[...] = w_ref[...].astype(bf16)
        if glu:
            wb3_ref[...] = w3_ref[...].astype(bf16)

    a = a_ref[...]
    y = jnp.dot(a, wb_ref[...], preferred_element_type=f32)
    if glu:
        y = jax.nn.silu(y) * jnp.dot(a, wb3_ref[...], preferred_element_type=f32)
    if has_res:
        y = r_ref[...] + y
    o_ref[...] = y.astype(o_ref.dtype)


def _matmul(a, w, lidx, n_out, tn, tm, w3=None, res=None, out_dtype=f32, name="matmul"):
    M, K = a.shape
    glu = w3 is not None
    has_res = res is not None
    w_spec = pl.BlockSpec((None, K, tn), lambda n, m: (lidx, 0, n))
    in_specs = [pl.BlockSpec((tm, K), lambda n, m: (m, 0)), w_spec]
    args = [a, w]
    scratch = [pltpu.VMEM((K, tn), bf16)]
    if glu:
        in_specs.append(w_spec)
        args.append(w3)
        scratch.append(pltpu.VMEM((K, tn), bf16))
    if has_res:
        in_specs.append(pl.BlockSpec((tm, tn), lambda n, m: (m, n)))
        args.append(res)
    return pl.pallas_call(
        functools.partial(_mm_kernel, glu=glu, has_res=has_res),
        grid=(pl.cdiv(n_out, tn), M // tm), in_specs=in_specs,
        out_specs=pl.BlockSpec((tm, tn), lambda n, m: (m, n)),
        out_shape=jax.ShapeDtypeStruct((M, n_out), out_dtype), scratch_shapes=scratch,
        compiler_params=_cparams(("arbitrary", "arbitrary")), name=name)(*args)


def _mm_out_kernel(*refs, n_a):
    a_refs = refs[:n_a]
    w_ref, h_ref, o_ref, wb_ref = refs[n_a:]

    @pl.when(pl.program_id(1) == 0)
    def _():
        wb_ref[...] = w_ref[...].astype(bf16)

    acc = h_ref[...]
    k0 = 0
    for a_ref in a_refs:
        kw = a_ref.shape[1]
        acc = acc + jnp.dot(a_ref[...].astype(bf16), wb_ref[k0:k0 + kw, :], preferred_element_type=f32)
        k0 += kw
    o_ref[...] = acc


def _mm_out(h, parts, row0, tm, w, lidx, tn=512):
    rows = parts[0].shape[0]
    K, N = w.shape[1], w.shape[2]
    rb0 = row0 // tm
    tile = pl.BlockSpec((tm, tn), lambda n, m: (rb0 + m, n))
    in_specs = [pl.BlockSpec((tm, a.shape[1]), lambda n, m: (m, 0)) for a in parts]
    in_specs += [pl.BlockSpec((None, K, tn), lambda n, m: (lidx, 0, n)), tile]
    return pl.pallas_call(
        functools.partial(_mm_out_kernel, n_a=len(parts)),
        grid=(N // tn, rows // tm), in_specs=in_specs, out_specs=tile,
        out_shape=jax.ShapeDtypeStruct(h.shape, h.dtype), scratch_shapes=[pltpu.VMEM((K, tn), bf16)],
        input_output_aliases={len(parts) + 1: 0},
        compiler_params=_cparams(("arbitrary", "arbitrary")), name="mm_out")(*parts, w, h)


def _hgrn_kernel(q_ref, f_ref, i_ref, g_ref, lb_ref, nw_ref, s0_ref, o_ref, s_out_ref,
                 st_s, qh_s, kh_s, gr_s, o_s, *, c, nt):
    t = pl.program_id(2)
    tb = q_ref.shape[0]
    lc = int(math.log2(c))

    @pl.when(t == 0)
    def _():
        st_s[...] = s0_ref[...].T

    lb = lb_ref[...]
    fl = f_ref[...]
    lf = jnp.logaddexp(jnp.log(lb), jnp.log1p(-lb) + jax.nn.log_sigmoid(fl))
    kh_s[...] = (1.0 - lb) * jax.nn.sigmoid(-fl)
    qh_s[...] = jax.nn.silu(q_ref[...])
    row = _iota((tb, tb), 0)
    col = _iota((tb, tb), 1)
    same = jnp.right_shift(row, lc) == jnp.right_shift(col, lc)
    bd = jnp.where(same, jnp.where(col <= row, 1.0, 0.0), 0.0).astype(bf16)
    gr_s[...] = _dotx(bd, lf, 1, 3)
    ones = jnp.ones((HG_D, HG_D), bf16)
    rows = _iota((c, HG_D), 0)

    nsb = tb // c
    incs, decs = [], []
    for j in range(nsb):
        r0 = j * c
        qs = qh_s[pl.ds(r0, c), :]
        ks = kh_s[pl.ds(r0, c), :]
        gs = gr_s[pl.ds(r0, c), :]
        vs = i_ref[pl.ds(r0, c), :]
        glast = gr_s[pl.ds(r0 + (c - 1), 1), :]
        xs = []
        for s in range(c):
            e = jnp.exp(gs - gs[s:s + 1, :])
            xs.append(jnp.where(rows >= s, qs * ks[s:s + 1, :] * e, 0.0))
        p = _dotx(jnp.concatenate(xs, axis=0), ones, 1, 1)
        o = p[0:c, :] * vs[0:1, :]
        for s in range(1, c):
            o = o + p[s * c:(s + 1) * c, :] * vs[s:s + 1, :]
        o_s[pl.ds(r0, c), :] = o
        incs.append(_bdot(vs, ks * jnp.exp(glast - gs), TN))
        decs.append(jnp.exp(glast))
    st = st_s[...]
    for j in range(nsb):
        r0 = j * c
        qt = qh_s[pl.ds(r0, c), :] * jnp.exp(gr_s[pl.ds(r0, c), :])
        o_s[pl.ds(r0, c), :] = o_s[pl.ds(r0, c), :] + _bdot(qt, st, NT)
        st = st * decs[j] + incs[j]

    st_s[...] = st
    o = o_s[...]
    y = o * lax.rsqrt(jnp.mean(o * o, axis=-1, keepdims=True) + EPS) * nw_ref[...]
    o_ref[...] = (y * jax.nn.silu(g_ref[...])).astype(o_ref.dtype)

    @pl.when(t == nt - 1)
    def _():
        s_out_ref[...] = st_s[...].T


def _hgrn(proj, row0, B, T, lb, nw, s0, c, tb, out_dtype):
    nt = T // tb
    rb0 = row0 // tb

    def col(k):
        return pl.BlockSpec((tb, HG_D), lambda b, h, t, k=k: (rb0 + b * nt + t, CB_HG + k * HG_H + h))

    st_spec = pl.BlockSpec((None, None, HG_D, HG_D), lambda b, h, t: (b, h, 0, 0))
    return pl.pallas_call(
        functools.partial(_hgrn_kernel, c=c, nt=nt),
        grid=(B, HG_H, nt),
        in_specs=[col(0), col(1), col(2), col(3),
                  pl.BlockSpec((None, 1, HG_D), lambda b, h, t: (h, 0, 0)),
                  pl.BlockSpec((1, HG_D), lambda b, h, t: (0, 0)),
                  st_spec],
        out_specs=[pl.BlockSpec((tb, HG_D), lambda b, h, t: (b * nt + t, h)), st_spec],
        out_shape=[jax.ShapeDtypeStruct((B * T, HG_W), out_dtype),
                   jax.ShapeDtypeStruct((B, HG_H, HG_D, HG_D), f32)],
        scratch_shapes=[pltpu.VMEM((HG_D, HG_D), f32)] + [pltpu.VMEM((tb, HG_D), f32)] * 4,
        compiler_params=_cparams(("parallel", "parallel", "arbitrary")), name="hgrn2",
    )(proj, proj, proj, proj, lb.reshape(HG_H, 1, HG_D), nw.reshape(1, HG_D), s0)


def _rwkv_kernel(*refs, c, nt, bb, parts):
    npr = C_RWKV // 256
    p_refs = refs[:npr * bb]
    (sh0_ref, s0_ref, mu_ref, w0_ref, w2_ref, a0_ref, a2_ref, g2_ref, kk_ref, ka_ref, rk_ref,
     lnw_ref, lnb_ref, bd1_ref) = refs[npr * bb:npr * bb + 14]
    o_ref, s_out_ref, sh_out_ref, st_s, carry_s = refs[npr * bb + 14:]
    t = pl.program_id(1)

    @pl.when(t == 0)
    def _():
        st_s[...] = s0_ref[...]
        carry_s[...] = sh0_ref[...]

    B_ = range(bb)
    H_ = range(RW_H)
    ps = [jnp.concatenate([r[...] for r in p_refs[bi * npr:(bi + 1) * npr]], axis=1) for bi in B_]
    xs = []
    for bi in B_:
        p = ps[bi]
        prev = jnp.concatenate([carry_s[bi], p[:-1]], axis=0)
        carry_s[bi] = p[c - 1:c]
        sh_out_ref[bi] = p[c - 1:c]
        xs.append(p + (prev - p) * mu_ref[...])
    o1 = RW_W
    o2 = o1 + LORA_W
    o3 = o2 + RW_W
    o4 = o3 + RW_W
    o5 = o4 + LORA_A
    r = [x[:, :o1] for x in xs]
    k = [x[:, o2:o3] for x in xs]
    v = [x[:, o3:o4] for x in xs]
    wl = [_bdot(jnp.tanh(x[:, o1:o2]), w2_ref[...]) for x in xs]
    al = [_bdot(x[:, o4:o5], a2_ref[...]) for x in xs]
    gate = [_bdot(jax.nn.sigmoid(x[:, o5:]), g2_ref[...]) for x in xs]
    lw = [-jnp.exp(-jax.nn.softplus(-(w0_ref[...] + w)) - 0.5) for w in wl]
    a = [jax.nn.sigmoid(a0_ref[...] + x) for x in al]
    bd1 = bd1_ref[...]
    kk = [x * kk_ref[...] for x in k]
    ssq = [_dotx(x * x, bd1, 2, 1) for x in kk]
    ri = _iota((c, c), 0)
    ci = _iota((c, c), 1)
    incl = ri >= ci
    strict = ri > ci
    tril = jnp.where(incl, 1.0, 0.0).astype(bf16)
    cum = [_dotx(tril, x, 1, 3) for x in lw]
    kk = [x / jnp.maximum(jnp.sqrt(s), 1e-12) for x, s in zip(kk, ssq)]
    k2 = [k[bi] * (1.0 + (a[bi] - 1.0) * ka_ref[...]) for bi in B_]
    beta = [kk[bi] * a[bi] for bi in B_]
    cum_c = [x[c - 1:c, :] for x in cum]
    e_neg = [jnp.exp(-x) for x in cum]
    e_suf = [jnp.exp(cum_c[bi] - cum[bi]) for bi in B_]
    a_bar = [-kk[bi] * jnp.exp(cum[bi] - lw[bi]) for bi in B_]
    r_bar = [r[bi] * jnp.exp(cum[bi]) for bi in B_]
    b_bar = [beta[bi] * e_neg[bi] for bi in B_]
    k_bar = [k2[bi] * e_neg[bi] for bi in B_]
    b_hat = [beta[bi] * e_suf[bi] for bi in B_]
    k_hat = [k2[bi] * e_suf[bi] for bi in B_]

    head = jnp.right_shift(_iota((c, RW_W), 1), 6)
    masks = [head == h for h in H_]
    stack = [jnp.concatenate([jnp.where(m, a_bar[bi], 0.0) for m in masks]
                             + [jnp.where(m, r_bar[bi], 0.0) for m in masks], axis=0).astype(bf16) for bi in B_]
    gb = [lax.dot_general(stack[bi], b_bar[bi].astype(bf16), NT, preferred_element_type=f32) for bi in B_]
    gk = [lax.dot_general(stack[bi], k_bar[bi].astype(bf16), NT, preferred_element_type=f32) for bi in B_]
    eye = jnp.where(ri == ci, 1.0, 0.0)
    vb = [x.astype(bf16) for x in v]
    bh = [(bi, h) for bi in B_ for h in H_]
    blk = lambda g, i: g[i * c:(i + 1) * c]
    lab = {(bi, h): jnp.where(strict, blk(gb[bi], h), 0.0) for bi, h in bh}
    lak = {(bi, h): jnp.where(strict, blk(gk[bi], h), 0.0) for bi, h in bh}
    prb = {(bi, h): jnp.where(incl, blk(gb[bi], RW_H + h), 0.0) for bi, h in bh}
    prk = {(bi, h): jnp.where(incl, blk(gk[bi], RW_H + h), 0.0) for bi, h in bh}
    t1 = {q: _bdot(lak[q], vb[q[0]]) for q in bh}
    y0p = {q: _bdot(prk[q], vb[q[0]]) for q in bh}
    inv = {q: eye + lab[q] for q in bh}
    pw = lab
    for _ in range(int(math.log2(c)) - 1):
        pw = {q: _dotx(pw[q], pw[q], parts, parts) for q in bh}
        inv = {q: inv[q] + _dotx(inv[q], pw[q], parts, parts) for q in bh}
    atp = {q: _dotx(inv[q], a_bar[q[0]], parts, parts) for q in bh}
    u0p = {q: _dotx(inv[q], t1[q], parts, parts) for q in bh}

    def fold(parts_):
        out = []
        for bi in B_:
            acc = jnp.where(masks[0], parts_[(bi, 0)], 0.0)
            for h in range(1, RW_H):
                acc = jnp.where(masks[h], parts_[(bi, h)], acc)
            out.append(acc)
        return out

    a_til = fold(atp)
    u0 = fold(u0p)
    y0 = fold(y0p)
    st = [st_s[bi] for bi in B_]
    stb = [x.astype(bf16) for x in st]
    u = [_bdot(a_til[bi], stb[bi], NT) + u0[bi] for bi in B_]
    yst = [_bdot(r_bar[bi], stb[bi], NT) for bi in B_]
    ub = [x.astype(bf16) for x in u]
    yu = fold({q: _bdot(prb[q], ub[q[0]]) for q in bh})
    upd = [_bdot(jnp.concatenate([u[bi], v[bi]], axis=0), jnp.concatenate([b_hat[bi], k_hat[bi]], axis=0), TN)
           for bi in B_]
    same = jnp.right_shift(_iota((RW_W, RW_W), 0), 6) == jnp.right_shift(_iota((RW_W, RW_W), 1), 6)
    for bi in B_:
        st_s[bi] = st[bi] * jnp.exp(cum_c[bi]) + jnp.where(same, upd[bi], 0.0)

    inv_n = 1.0 / RW_N
    y = [yst[bi] + y0[bi] + yu[bi] for bi in B_]
    mean = [_dotx(x, bd1, 2, 1) * inv_n for x in y]
    bsum = [_dotx(r[bi] * k2[bi] * rk_ref[...], bd1, 2, 1) for bi in B_]
    d = [y[bi] - mean[bi] for bi in B_]
    var = [_dotx(x * x, bd1, 2, 1) * inv_n for x in d]
    for bi in B_:
        yn = d[bi] * lax.rsqrt(var[bi] + RW_GN_EPS) * lnw_ref[...] + lnb_ref[...]
        o_ref[bi] = ((yn + bsum[bi] * v[bi]) * gate[bi]).astype(o_ref.dtype)

    @pl.when(t == nt - 1)
    def _():
        s_out_ref[...] = st_s[...]


def _rwkv(proj, row0, B, T, c, bb, parts, sh0, s0_bd, lw, out_dtype):
    nt = T // c
    rb0 = row0 // c
    pw = 256
    G = B // bb
    vec = lambda n: pl.BlockSpec((1, n), lambda g, t: (0, 0))
    mat = lambda r_, n: pl.BlockSpec((r_, n), lambda g, t: (0, 0))
    p_specs = [pl.BlockSpec((c, pw), lambda g, t, j=j, i=i: (rb0 + (g * bb + i) * nt + t, (CB_RW * LANES) // pw + j))
               for i in range(bb) for j in range(C_RWKV // pw)]
    hh = jnp.arange(RW_W) // RW_N
    bd1 = (hh[:, None] == hh[None, :]).astype(bf16)
    outs = pl.pallas_call(
        functools.partial(_rwkv_kernel, c=c, nt=nt, bb=bb, parts=parts),
        grid=(G, nt),
        in_specs=p_specs + [
            pl.BlockSpec((bb, 1, C_RWKV), lambda g, t: (g, 0, 0)),
            pl.BlockSpec((bb, RW_W, RW_W), lambda g, t: (g, 0, 0)),
            vec(C_RWKV), vec(RW_W), mat(LORA_W, RW_W), vec(RW_W), mat(LORA_A, RW_W), mat(LORA_G, RW_W),
            vec(RW_W), vec(RW_W), vec(RW_W), vec(RW_W), vec(RW_W), mat(RW_W, RW_W)],
        out_specs=[pl.BlockSpec((bb, c, RW_W), lambda g, t: (g, t, 0)),
                   pl.BlockSpec((bb, RW_W, RW_W), lambda g, t: (g, 0, 0)),
                   pl.BlockSpec((bb, 1, C_RWKV), lambda g, t: (g, 0, 0))],
        out_shape=[jax.ShapeDtypeStruct((B, T, RW_W), out_dtype),
                   jax.ShapeDtypeStruct((B, RW_W, RW_W), f32), jax.ShapeDtypeStruct((B, 1, C_RWKV), f32)],
        scratch_shapes=[pltpu.VMEM((bb, RW_W, RW_W), f32), pltpu.VMEM((bb, 1, C_RWKV), f32)],
        compiler_params=_cparams(("parallel", "arbitrary")), name="rwkv7",
    )(*([proj] * (bb * (C_RWKV // pw))), sh0.reshape(B, 1, C_RWKV), s0_bd,
      lw['mu'].reshape(1, -1), lw['w0'].reshape(1, -1), lw['w2'], lw['a0'].reshape(1, -1), lw['a2'], lw['g2'],
      lw['kk'].reshape(1, -1), lw['ka'].reshape(1, -1), lw['rk'].reshape(1, -1),
      lw['lnx_w'].reshape(1, -1), lw['lnx_b'].reshape(1, -1), bd1)
    return outs[0].reshape(B * T, RW_W), outs[1], outs[2]


def _to_blockdiag(s):
    B = s.shape[0]
    eye = jnp.eye(RW_H, dtype=s.dtype)
    return jnp.einsum('bhvk,hg->bhvgk', s, eye).reshape(B, RW_W, RW_W)


def _from_blockdiag(s):
    B = s.shape[0]
    s5 = s.reshape(B, RW_H, RW_N, RW_H, RW_N)
    idx = jnp.arange(RW_H)
    return s5[:, idx, :, idx, :].transpose(1, 0, 2, 3)


def _fox_prep_kernel(q_ref, k_ref, v_ref, qg_ref, kg_ref, qn_ref, kn_ref, kb_ref, vn_ref, vb_ref):
    def nrm(x, g):
        return x * lax.rsqrt(jnp.mean(x * x, axis=-1, keepdims=True) + EPS) * g
    qn_ref[...] = nrm(q_ref[...], qg_ref[...]).astype(qn_ref.dtype)
    kn = nrm(k_ref[...], kg_ref[...])
    kn_ref[...] = kn
    kb_ref[...] = kn.astype(kb_ref.dtype)
    v = v_ref[...]
    vn_ref[...] = v
    vb_ref[...] = v.astype(vb_ref.dtype)


def _fox_prep(proj, row0, rows, tt, qg, kg, lowp):
    rb0 = row0 // tt
    col = lambda cb: pl.BlockSpec((tt, FX_DH), lambda r, h, cb=cb: (rb0 + r, cb + h))
    out = pl.BlockSpec((tt, FX_DH), lambda r, h: (r, h))
    gsp = pl.BlockSpec((1, FX_DH), lambda r, h: (0, 0))
    sds = lambda dt: jax.ShapeDtypeStruct((rows, FX_W), dt)
    return pl.pallas_call(
        _fox_prep_kernel, grid=(rows // tt, FX_H),
        in_specs=[col(CB_FQ), col(CB_FK), col(CB_FV), gsp, gsp],
        out_specs=[out] * 5, out_shape=[sds(lowp), sds(f32), sds(lowp), sds(f32), sds(lowp)],
        compiler_params=_cparams(("parallel", "parallel")), name="fox_prep",
    )(proj, proj, proj, qg.reshape(1, FX_DH), kg.reshape(1, FX_DH))


def _fox_lf_kernel(ff_ref, b_ref, lf_ref, cum_ref, carry_s):
    tt = ff_ref.shape[0]

    @pl.when(pl.program_id(1) == 0)
    def _():
        carry_s[...] = jnp.zeros_like(carry_s)

    lf = jax.nn.log_sigmoid(ff_ref[...] + b_ref[...])
    tril = jnp.where(_iota((tt, tt), 0) >= _iota((tt, tt), 1), 1.0, 0.0).astype(bf16)
    cum = _dotx(tril, lf, 1, 3) + carry_s[...]
    carry_s[...] = cum[tt - 1:tt]
    lf_ref[...] = lf[:, :FX_H]
    cum_ref[...] = cum[:, :FX_H]


def _fox_lf(ffl, row0, B, T, tt, fbias):
    nt = T // tt
    rb0 = row0 // tt
    bias = jnp.zeros((1, LANES), f32).at[0, :FX_H].set(fbias)
    out = pl.BlockSpec((tt, FX_H), lambda b, t: (b * nt + t, 0))
    return pl.pallas_call(
        _fox_lf_kernel, grid=(B, nt),
        in_specs=[pl.BlockSpec((tt, LANES), lambda b, t: (rb0 + b * nt + t, 0)),
                  pl.BlockSpec((1, LANES), lambda b, t: (0, 0))],
        out_specs=[out, out], out_shape=[jax.ShapeDtypeStruct((B * T, FX_H), f32)] * 2,
        scratch_shapes=[pltpu.VMEM((1, LANES), f32)],
        compiler_params=_cparams(("parallel", "arbitrary")), name="fox_logf",
    )(ffl, bias)


def _fox_attn_kernel(qt_ref, kt_ref, q_ref, k_ref, v_ref, cq_ref, ck_ref, o_ref, m_s, l_s, acc_s, *, tq, tk):
    step = pl.program_id(2)
    qi = qt_ref[step]
    ki = kt_ref[step]

    @pl.when(ki == 0)
    def _():
        m_s[...] = jnp.full_like(m_s, -jnp.inf)
        l_s[...] = jnp.zeros_like(l_s)
        acc_s[...] = jnp.zeros_like(acc_s)

    nh = q_ref.shape[1] // FX_DH

    def block(masked):
        hs = [slice(i * FX_DH, (i + 1) * FX_DH) for i in range(nh)]
        ss = [lax.dot_general(q_ref[:, hs[i]], k_ref[:, hs[i]], NT, preferred_element_type=f32) for i in range(nh)]
        ps = []
        for i in range(nh):
            s = ss[i] * (FX_DH ** -0.5) + (cq_ref[i] - ck_ref[i])
            if masked:
                qpos = qi * tq + _iota((tq, tk), 0)
                kpos = ki * tk + _iota((tq, tk), 1)
                s = jnp.where(kpos <= qpos, s, -jnp.inf)
            m_old = m_s[i]
            m_new = jnp.maximum(m_old, jnp.max(s, axis=-1, keepdims=True))
            alpha = jnp.exp(m_old - m_new)
            p = jnp.exp(s - m_new)
            l_s[i] = alpha * l_s[i] + jnp.sum(p, axis=-1, keepdims=True)
            acc_s[:, hs[i]] = alpha * acc_s[:, hs[i]]
            m_s[i] = m_new
            ps.append(p.astype(bf16))
        for i in range(nh):
            acc_s[:, hs[i]] = acc_s[:, hs[i]] + jnp.dot(ps[i], v_ref[:, hs[i]], preferred_element_type=f32)

    on_diagonal = ki == qi
    pl.when(jnp.logical_not(on_diagonal))(functools.partial(block, False))
    pl.when(on_diagonal)(functools.partial(block, True))

    @pl.when(on_diagonal)
    def _():
        for i in range(nh):
            hs = slice(i * FX_DH, (i + 1) * FX_DH)
            o_ref[:, hs] = (acc_s[:, hs] / l_s[i]).astype(o_ref.dtype)


def _fox_attn(qn, kb, vb, cum, B, T, tq, nh=2):
    tk = tq
    nq = T // tq
    cq = cum.reshape(B, T, FX_H).transpose(0, 2, 1).reshape(B, FX_H, T, 1)
    ck = cq.reshape(B, FX_H, 1, T)
    pairs = [(q, k) for q in range(nq) for k in range(q + 1)]
    qt = jnp.asarray([p[0] for p in pairs], i32)
    kt = jnp.asarray([p[1] for p in pairs], i32)
    w = nh * FX_DH
    kv = pl.BlockSpec((tk, w), lambda b, h, s, qt_, kt_: (b * nq + kt_[s], h))
    grid_spec = pltpu.PrefetchScalarGridSpec(
        num_scalar_prefetch=2, grid=(B, FX_H // nh, len(pairs)),
        in_specs=[pl.BlockSpec((tq, w), lambda b, h, s, qt_, kt_: (b * nq + qt_[s], h)), kv, kv,
                  pl.BlockSpec((None, nh, tq, 1), lambda b, h, s, qt_, kt_: (b, h, qt_[s], 0)),
                  pl.BlockSpec((None, nh, 1, tk), lambda b, h, s, qt_, kt_: (b, h, 0, kt_[s]))],
        out_specs=pl.BlockSpec((tq, w), lambda b, h, s, qt_, kt_: (b * nq + qt_[s], h)),
        scratch_shapes=[pltpu.VMEM((nh, tq, 1), f32), pltpu.VMEM((nh, tq, 1), f32), pltpu.VMEM((tq, w), f32)])
    return pl.pallas_call(
        functools.partial(_fox_attn_kernel, tq=tq, tk=tk), grid_spec=grid_spec,
        out_shape=jax.ShapeDtypeStruct((B * T, FX_W), bf16),
        compiler_params=_cparams(("parallel", "parallel", "arbitrary")), name="fox_attn",
    )(qt, kt, qn, kb, vb, cq, ck)


def _rep_rows(x, td):
    return jnp.concatenate([jnp.broadcast_to(x[h:h + 1], (td, x.shape[1])) for h in range(x.shape[0])], axis=0)


def _fox_dec_kernel(pt_ref, q_ref, kn_ref, vn_ref, cqc_ref, cqr_ref, *refs, td, n_steps, pps, nb):
    per = 3 * pps
    kc_refs = [refs[b * per:b * per + pps] for b in range(nb)]
    vc_refs = [refs[b * per + pps:b * per + 2 * pps] for b in range(nb)]
    lft_refs = [refs[b * per + 2 * pps:b * per + 3 * pps] for b in range(nb)]
    o_ref, m_s, l_s, acc_s, suf_s = refs[nb * per:]
    j = pl.program_id(1)
    nr = FX_H * td
    scale = FX_DH ** -0.5
    B_ = range(nb)
    H_ = range(FX_H)
    hcols = [slice(h * FX_DH, (h + 1) * FX_DH) for h in H_]
    hrows = [slice(h * td, (h + 1) * td) for h in H_]
    brows = [slice(b * td, (b + 1) * td) for b in B_]

    def heads_q(b):
        q = q_ref[brows[b], :]
        return [q[:, hcols[h]].astype(bf16) for h in H_]

    def softmax_step(b, s):
        m_old = m_s[b]
        m_new = jnp.maximum(m_old, jnp.max(s, axis=-1, keepdims=True))
        alpha = jnp.exp(m_old - m_new)
        p = jnp.exp(s - m_new)
        l_s[b] = alpha * l_s[b] + jnp.sum(p, axis=-1, keepdims=True)
        m_s[b] = m_new
        return alpha, p

    def apply_values(b, alpha, p, v_of):
        pv = []
        for h in H_:
            acc = None
            for i in range(p.shape[1] // PAGE):
                t = jnp.dot(p[hrows[h], i * PAGE:(i + 1) * PAGE].astype(bf16), v_of(i, h), preferred_element_type=f32)
                acc = t if acc is None else acc + t
            pv.append(acc)
        acc_s[b] = alpha * acc_s[b] + jnp.concatenate(pv, axis=0)

    @pl.when(j == 0)
    def _():
        pad = jnp.zeros((PAGE - td, FX_W), f32)
        m_s[...] = jnp.full_like(m_s, -jnp.inf)
        l_s[...] = jnp.zeros_like(l_s)
        acc_s[...] = jnp.zeros_like(acc_s)
        suf_s[...] = jnp.zeros_like(suf_s)
        kpos = _iota((nr, PAGE), 1)
        qpos = jnp.bitwise_and(_iota((nr, PAGE), 0), td - 1)
        ss, vps = [], []
        for b in B_:
            qh = heads_q(b)
            kp = jnp.concatenate([kn_ref[brows[b], :], pad], axis=0).astype(bf16)
            vps.append(jnp.concatenate([vn_ref[brows[b], :], pad], axis=0).astype(bf16))
            s = jnp.concatenate([lax.dot_general(qh[h], kp[:, hcols[h]], NT, preferred_element_type=f32)
                                 for h in H_], axis=0) * scale
            s = s + (cqc_ref[b] - _rep_rows(cqr_ref[b], td))
            ss.append(jnp.where(kpos <= qpos, s, -jnp.inf))
        ap = [softmax_step(b, ss[b]) for b in B_]
        for b in B_:
            apply_values(b, ap[b][0], ap[b][1], lambda i, h, b=b: vps[b][:, hcols[h]])

    @pl.when(j > 0)
    def _():
        later = jnp.where(_iota((PAGE, PAGE), 0) > _iota((PAGE, PAGE), 1), 1.0, 0.0).astype(bf16)
        ones = jnp.ones((PAGE, PAGE), bf16)
        ss = []
        for b in B_:
            qh = heads_q(b)
            cq = cqc_ref[b]
            after = suf_s[b]
            parts = []
            for i in range(pps):
                lf = lft_refs[b][i][...]
                suf = _dotx(lf, later, 3, 1, TN) + after
                after = after + _dotx(lf, ones, 3, 1, TN)
                s = jnp.concatenate(
                    [lax.dot_general(qh[h], kc_refs[b][i][pl.ds(h, PAGE, stride=FX_H), :].astype(bf16), NT,
                                     preferred_element_type=f32) for h in H_], axis=0) * scale
                parts.append(s + (cq + _rep_rows(suf, td)))
            suf_s[b] = after
            ss.append(jnp.concatenate(parts, axis=1))
        ap = [softmax_step(b, ss[b]) for b in B_]
        for b in B_:
            apply_values(b, ap[b][0], ap[b][1],
                         lambda i, h, b=b: vc_refs[b][i][pl.ds(h, PAGE, stride=FX_H), :].astype(bf16))

    @pl.when(j == n_steps - 1)
    def _():
        for b in B_:
            a = acc_s[b] / l_s[b]
            for h in H_:
                o_ref[brows[b], hcols[h]] = a[hrows[h]].astype(o_ref.dtype)


def _fox_decode(qn, kn, vn, cnew, kcache, vcache, lft, page_table, layer, td):
    Bd, n_pages = page_table.shape
    pps = 4 if n_pages % 4 == 0 else (2 if n_pages % 2 == 0 else 1)
    nb = 2 if Bd % 2 == 0 else 1
    n_steps = n_pages // pps + 1
    nr = FX_H * td
    c3 = cnew.reshape(Bd, td, FX_H).transpose(0, 2, 1)
    cqc = c3.reshape(Bd, nr, 1)
    cqr = jnp.zeros((Bd, FX_H, PAGE), f32).at[:, :, :td].set(c3)

    def page(b, i, nd):
        return lambda g, j, pt: (layer, pt[g * nb + b, n_pages - 1 - i - (jnp.maximum(j, 1) - 1) * pps]) + (0,) * nd

    tok = pl.BlockSpec((nb * td, FX_W), lambda g, j, pt: (g, 0))
    pages = []
    for b in range(nb):
        kvs = [pl.BlockSpec((None, None, PAGE * FX_H, FX_DH), page(b, i, 2)) for i in range(pps)]
        pages += kvs + kvs + [pl.BlockSpec((None, None, PAGE, FX_H), page(b, i, 2)) for i in range(pps)]
    grid_spec = pltpu.PrefetchScalarGridSpec(
        num_scalar_prefetch=1, grid=(Bd // nb, n_steps),
        in_specs=[tok, tok, tok,
                  pl.BlockSpec((nb, nr, 1), lambda g, j, pt: (g, 0, 0)),
                  pl.BlockSpec((nb, FX_H, PAGE), lambda g, j, pt: (g, 0, 0))] + pages,
        out_specs=pl.BlockSpec((nb * td, FX_W), lambda g, j, pt: (g, 0)),
        scratch_shapes=[pltpu.VMEM((nb, nr, 1), f32), pltpu.VMEM((nb, nr, 1), f32), pltpu.VMEM((nb, nr, FX_DH), f32),
                        pltpu.VMEM((nb, FX_H, PAGE), f32)])
    return pl.pallas_call(
        functools.partial(_fox_dec_kernel, td=td, n_steps=n_steps, pps=pps, nb=nb), grid_spec=grid_spec,
        out_shape=jax.ShapeDtypeStruct((Bd * td, FX_W), f32),
        compiler_params=_cparams(("parallel", "arbitrary")), name="fox_decode",
    )(page_table, qn, kn, vn, cqc, cqr, *(([kcache] * pps + [vcache] * pps + [lft] * pps) * nb))


def _moe_plan_kernel(r_ref, pos_ref, te_ref, cs_s, *, rb, tm_e):
    M = r_ref.shape[0]
    nb = M // rb
    lane = _iota((rb, LANES), 1)
    strict = jnp.where(_iota((rb, rb), 0) > _iota((rb, rb), 1), 1.0, 0.0).astype(bf16)

    def onehots(blk):
        r = r_ref[blk * rb:(blk + 1) * rb, :]
        i1 = r[:, 0:1].astype(i32)
        i2 = r[:, 1:2].astype(i32)
        return lane == i1, lane == i2

    carry = jnp.zeros((1, LANES), f32)
    for blk in range(nb):
        o1, o2 = onehots(blk)
        oh = jnp.where(o1, 1.0, jnp.where(o2, 1.0, 0.0))
        cs_s[blk * rb:(blk + 1) * rb, :] = _dotx(strict, oh.astype(bf16), 1, 1) + carry
        carry = carry + jnp.sum(oh, axis=0, keepdims=True)
    padded = jnp.floor((carry + (tm_e - 1)) * (1.0 / tm_e)) * tm_e
    before = jnp.where(_iota((LANES, LANES), 0) < _iota((LANES, LANES), 1), 1.0, 0.0).astype(bf16)
    off = _dotx(jnp.broadcast_to(padded, (8, LANES)), before, 3, 1)[0:1]
    end = off + padded
    for blk in range(nb):
        o1, o2 = onehots(blk)
        base = off + cs_s[blk * rb:(blk + 1) * rb, :]
        p1 = jnp.sum(jnp.where(o1, base, 0.0), axis=-1, keepdims=True)
        p2 = jnp.sum(jnp.where(o2, base, 0.0), axis=-1, keepdims=True)
        pos_ref[blk * rb:(blk + 1) * rb, :] = jnp.where(lane == 0, p1, jnp.where(lane == 1, p2, 0.0)).astype(i32)
    l1 = _iota((1, LANES), 1)
    start = (l1 * tm_e).astype(f32)
    te = jnp.zeros((1, LANES), f32)
    for e in range(N_EXPERTS):
        end_e = jnp.sum(jnp.where(l1 == e, end, 0.0), axis=-1, keepdims=True)
        te = te + jnp.where(end_e <= start, 1.0, 0.0)
    total = jnp.sum(jnp.where(l1 == N_EXPERTS - 1, end, 0.0), axis=-1, keepdims=True)
    te = jnp.minimum(te, N_EXPERTS - 1.0)
    te_ref[...] = jnp.where(l1 == LANES - 1, total * (1.0 / tm_e), te).astype(i32)


def _moe_plan(route, tm_e, rb):
    M = route.shape[0]
    return pl.pallas_call(
        functools.partial(_moe_plan_kernel, rb=rb, tm_e=tm_e),
        out_shape=[jax.ShapeDtypeStruct((M, LANES), i32), jax.ShapeDtypeStruct((1, LANES), i32)],
        scratch_shapes=[pltpu.VMEM((M, LANES), f32)],
        compiler_params=pltpu.CompilerParams(vmem_limit_bytes=VMEM_LIMIT_BYTES), name="moe_plan")(route)


def _moe_gather_kernel(pos_ref, te_ref, u_ref, xs_ref, src_s, buf, sem, *, tm_e, n_pairs):
    t = pl.program_id(0)
    rows = src_s.shape[0]

    @pl.when(t == 0)
    def _():
        def clear(i, c):
            src_s[i] = 0
            return c

        def put(i, c):
            src_s[pos_ref[i]] = jnp.right_shift(i, 1)
            return c

        lax.fori_loop(0, rows, clear, 0, unroll=8)
        lax.fori_loop(0, n_pairs, put, 0, unroll=8)

    ns = u_ref.shape[1]

    def copy(r):
        return pltpu.make_async_copy(u_ref.at[src_s[t * tm_e + r]], buf.at[pl.ds(pl.multiple_of(r * ns, ns), ns)], sem)

    @pl.when(t < te_ref[LANES - 1])
    def _():
        def start(r, c):
            copy(r).start()
            return c

        def wait(r, c):
            copy(r).wait()
            return c

        lax.fori_loop(0, tm_e, start, 0, unroll=8)
        lax.fori_loop(0, tm_e, wait, 0, unroll=8)
        for s in range(ns):
            xs_ref[:, s * LANES:(s + 1) * LANES] = buf[pl.ds(s, tm_e, stride=ns), :].astype(xs_ref.dtype)

    @pl.when(t >= te_ref[LANES - 1])
    def _():
        xs_ref[...] = jnp.zeros_like(xs_ref)


def _moe_gather(u3, pos_flat, te, rows, tm_e):
    M, S, _ = u3.shape
    grid_spec = pltpu.PrefetchScalarGridSpec(
        num_scalar_prefetch=2, grid=(rows // tm_e,),
        in_specs=[pl.BlockSpec(memory_space=pl.ANY)],
        out_specs=pl.BlockSpec((tm_e, S * LANES), lambda t, p, te_: (t, 0)),
        scratch_shapes=[pltpu.SMEM((rows,), i32), pltpu.VMEM((tm_e * S, LANES), f32), pltpu.SemaphoreType.DMA(())])
    return pl.pallas_call(
        functools.partial(_moe_gather_kernel, tm_e=tm_e, n_pairs=2 * M), grid_spec=grid_spec,
        out_shape=jax.ShapeDtypeStruct((rows, S * LANES), bf16),
        compiler_params=_cparams(("arbitrary",)), name="moe_gather")(pos_flat, te, u3)


def _moe_mm_kernel(te_ref, *refs, glu):
    it = iter(refs)
    a_ref = next(it)
    w_ref = next(it)
    w3_ref = next(it) if glu else None
    o_ref = next(it)
    wb_ref = next(it)
    wb3_ref = next(it) if glu else None
    t = pl.program_id(1)
    e = te_ref[t]
    e_prev = te_ref[jnp.maximum(t - 1, 0)]

    @pl.when((t == 0) | (e != e_prev))
    def _():
        wb_ref[...] = w_ref[...].astype(bf16)
        if glu:
            wb3_ref[...] = w3_ref[...].astype(bf16)

    @pl.when(t < te_ref[LANES - 1])
    def _():
        a = a_ref[...]
        y = jnp.dot(a, wb_ref[...], preferred_element_type=f32)
        if glu:
            y = jax.nn.silu(y) * jnp.dot(a, wb3_ref[...], preferred_element_type=f32)
        if len(o_ref.shape) == 3:
            for s in range(o_ref.shape[1]):
                o_ref[:, s, :] = y[:, s * LANES:(s + 1) * LANES].astype(o_ref.dtype)
        else:
            o_ref[...] = y.astype(o_ref.dtype)

    @pl.when(t >= te_ref[LANES - 1])
    def _():
        o_ref[...] = jnp.zeros_like(o_ref)


def _moe_matmul(a, te, w, fi, tm_e, tn, w3=None, out_dtype=f32, slabs=False, name="moe_matmul"):
    R, K = a.shape
    N = w.shape[-1]
    glu = w3 is not None
    w_spec = pl.BlockSpec((None, None, K, tn), lambda n, t, te_: (fi, te_[t], 0, n))
    in_specs = [pl.BlockSpec((tm_e, K), lambda n, t, te_: (t, 0)), w_spec] + ([w_spec] if glu else [])
    args = [a, w] + ([w3] if glu else [])
    scratch = [pltpu.VMEM((K, tn), bf16)] * (2 if glu else 1)
    if slabs:
        out_shape = jax.ShapeDtypeStruct((R, N // tn, tn // LANES, LANES), out_dtype)
        out_spec = pl.BlockSpec((tm_e, None, tn // LANES, LANES), lambda n, t, te_: (t, n, 0, 0))
    else:
        out_shape = jax.ShapeDtypeStruct((R, N), out_dtype)
        out_spec = pl.BlockSpec((tm_e, tn), lambda n, t, te_: (t, n))
    grid_spec = pltpu.PrefetchScalarGridSpec(
        num_scalar_prefetch=1, grid=(N // tn, R // tm_e), in_specs=in_specs,
        out_specs=out_spec, scratch_shapes=scratch)
    return pl.pallas_call(
        functools.partial(_moe_mm_kernel, glu=glu), grid_spec=grid_spec, out_shape=out_shape,
        compiler_params=_cparams(("arbitrary", "arbitrary")), name=name)(te, *args)


def _moe_combine_kernel(pos_ref, h_ref, r_ref, y_ref, o_ref, buf0, buf1, sem, *, tb):
    i0 = pl.program_id(0) * tb
    bufs = (buf0, buf1)
    ns = y_ref.shape[1]

    def copy(i, j):
        return pltpu.make_async_copy(y_ref.at[pos_ref[2 * (i0 + i) + j]],
                                     bufs[j].at[pl.ds(pl.multiple_of(i * ns, ns), ns)], sem)

    def start(i, c):
        copy(i, 0).start()
        copy(i, 1).start()
        return c

    def wait(i, c):
        copy(i, 0).wait()
        copy(i, 1).wait()
        return c

    lax.fori_loop(0, tb, start, 0, unroll=8)
    lax.fori_loop(0, tb, wait, 0, unroll=8)
    r = r_ref[...]
    g1 = r[:, 2:3]
    g2 = r[:, 3:4]
    for s in range(ns):
        c0 = s * LANES
        y0 = buf0[pl.ds(s, tb, stride=ns), :]
        y1 = buf1[pl.ds(s, tb, stride=ns), :]
        o_ref[:, c0:c0 + LANES] = h_ref[:, c0:c0 + LANES] + (g1 * y0 + g2 * y1)


def _moe_combine(h, route, y3, pos_flat, tb):
    M, D = h.shape
    grid_spec = pltpu.PrefetchScalarGridSpec(
        num_scalar_prefetch=1, grid=(M // tb,),
        in_specs=[pl.BlockSpec((tb, D), lambda m, p: (m, 0)), pl.BlockSpec((tb, LANES), lambda m, p: (m, 0)),
                  pl.BlockSpec(memory_space=pl.ANY)],
        out_specs=pl.BlockSpec((tb, D), lambda m, p: (m, 0)),
        scratch_shapes=[pltpu.VMEM((tb * y3.shape[1], LANES), f32), pltpu.VMEM((tb * y3.shape[1], LANES), f32),
                        pltpu.SemaphoreType.DMA(())])
    return pl.pallas_call(
        functools.partial(_moe_combine_kernel, tb=tb), grid_spec=grid_spec,
        out_shape=jax.ShapeDtypeStruct((M, D), f32),
        compiler_params=_cparams(("arbitrary",)), name="moe_combine")(pos_flat, h, route, y3)


def _moe(h, u3, route, w1, w3, w2, fi, tm, tm_e=256):
    M = h.shape[0]
    rows = (pl.cdiv(2 * M, tm_e) + N_EXPERTS) * tm_e
    pos, te = _moe_plan(route, tm_e, tm)
    pos_flat = pos[:, :2].reshape(-1)
    te = te.reshape(-1)
    xs = _moe_gather(u3, pos_flat, te, rows, tm_e)
    act = _moe_matmul(xs, te, w1, fi, tm_e, 1024, w3=w3, out_dtype=bf16, name="moe_up")
    y = _moe_matmul(act, te, w2, fi, tm_e, 512, slabs=True, name="moe_down")
    y3 = y.reshape(rows, h.shape[1] // LANES, LANES)
    return _moe_combine(h, route, y3, pos_flat, tm // 2)


def kernel(x_prompt, x_sample, cache_fox_k, cache_fox_v, cache_fox_logf, page_table, state_hgrn, state_rwkv, state_rwkv_shift, norm_mix, w_in, w_out, hgrn_lb, hgrn_norm, rwkv_mu, rwkv_w0, rwkv_w2, rwkv_a0, rwkv_a2, rwkv_g2, rwkv_kk, rwkv_ka, rwkv_rk, rwkv_lnx_w, rwkv_lnx_b, fox_qnorm, fox_knorm, fox_fbias, norm_ffn, ffn_w1, ffn_w3, ffn_w2, moe_router, moe_w1, moe_w3, moe_w2):
    B, T, D = x_prompt.shape
    Bd, Td, _ = x_sample.shape
    depth = w_in.shape[0]
    n_pool = cache_fox_k.shape[1]
    Mp, Ms = B * T, Bd * Td
    M = Mp + Ms
    tm = 688 if M % 688 == 0 else M
    d_ff = ffn_w1.shape[-1]

    h = jnp.concatenate([x_prompt.reshape(Mp, D), x_sample.reshape(Ms, D)], axis=0)
    lb_all = jnp.cumsum(jax.nn.softmax(hgrn_lb.astype(f32), axis=0), axis=0)
    lb_all = lb_all - lb_all[:1]
    kcache = cache_fox_k.reshape(depth, n_pool, PAGE * FX_H, FX_DH)
    vcache = cache_fox_v.reshape(depth, n_pool, PAGE * FX_H, FX_DH)
    lft = cache_fox_logf
    zeros_hg = jnp.zeros((B, HG_H, HG_D, HG_D), f32)
    zeros_rw = jnp.zeros((B, RW_W, RW_W), f32)
    zeros_sh = jnp.zeros((B, C_RWKV), f32)
    c_p = 64 if T % 64 == 0 else T
    tb_p = 256 if T % 256 == 0 else T
    tt_p = 512 if T % 512 == 0 else T

    outs = [[] for _ in range(12)]
    for l in range(depth):
        fi = l // 2
        rw = dict(mu=rwkv_mu[l], w0=rwkv_w0[l], w2=rwkv_w2[l], a0=rwkv_a0[l], a2=rwkv_a2[l], g2=rwkv_g2[l],
                  kk=rwkv_kk[l], ka=rwkv_ka[l], rk=rwkv_rk[l], lnx_w=rwkv_lnx_w[l], lnx_b=rwkv_lnx_b[l])
        u, ffl = _rmsnorm(h, norm_mix[l], w_small=w_in[l][:, N_PROJ:], tm=tm)
        proj = _matmul(u, w_in, l, N_PROJ, 768, tm, name="mm_in")

        o_hg, s_hg = _hgrn(proj, 0, B, T, lb_all[l], hgrn_norm[l], zeros_hg, 16 if tb_p % 16 == 0 else tb_p, tb_p, bf16)
        o_rw, s_rw, sh = _rwkv(proj, 0, B, T, c_p, B, 1, zeros_sh, zeros_rw, rw, bf16)
        qn, kn, kb, vn, vb = _fox_prep(proj, 0, Mp, min(1024, Mp), fox_qnorm[l], fox_knorm[l], bf16)
        lf, cum = _fox_lf(ffl, 0, B, T, tt_p, fox_fbias[l])
        o_fx = _fox_attn(qn, kb, vb, cum, B, T, tt_p)
        for i, x in enumerate((kn.reshape(B, T, FX_H, FX_DH), vn.reshape(B, T, FX_H, FX_DH),
                               lf.reshape(B, T, FX_H), s_hg, _from_blockdiag(s_rw), sh.reshape(B, C_RWKV))):
            outs[i].append(x)

        o_hg_s, s_hg_s = _hgrn(proj, Mp, Bd, Td, lb_all[l], hgrn_norm[l], state_hgrn[l], Td, Td, f32)
        o_rw_s, s_rw_s, sh_s = _rwkv(proj, Mp, Bd, Td, Td, 1, 2, state_rwkv_shift[l], _to_blockdiag(state_rwkv[l]),
                                     rw, f32)
        qn_s, kn_s, _, vn_s, _ = _fox_prep(proj, Mp, Ms, Ms, fox_qnorm[l], fox_knorm[l], f32)
        lf_s, cnew = _fox_lf(ffl, Mp, Bd, Td, Td, fox_fbias[l])
        o_fx_s = _fox_decode(qn_s, kn_s, vn_s, cnew, kcache, vcache, lft, page_table, l, Td)
        for i, x in enumerate((kn_s.reshape(Bd, Td, FX_H, FX_DH), vn_s.reshape(Bd, Td, FX_H, FX_DH),
                               lf_s.reshape(Bd, Td, FX_H), s_hg_s, _from_blockdiag(s_rw_s), sh_s.reshape(Bd, C_RWKV))):
            outs[6 + i].append(x)

        h = _mm_out(h, [o_hg, o_rw, o_fx], 0, min(512, Mp), w_out, l)
        h = _mm_out(h, [o_hg_s, o_rw_s, o_fx_s], Mp, Ms, w_out, l)
        if l % 2 == 0:
            u2 = _rmsnorm(h, norm_ffn[l], tm=tm)
            act = _matmul(u2, ffn_w1, fi, d_ff, 512, tm, w3=ffn_w3, out_dtype=bf16, name="ffn_up")
            h = _matmul(act, ffn_w2, fi, D, 512, tm, res=h, name="ffn_down")
        else:
            u3, route = _rmsnorm(h, norm_ffn[l], w_small=moe_router[fi], route=True, tm=tm, slabs=True)
            h = _moe(h, u3, route, moe_w1, moe_w3, moe_w2, fi, tm)

    st = [jnp.stack(o) for o in outs]
    return (h[:Mp].reshape(B, T, D), h[Mp:].reshape(Bd, Td, D), *st)
```
